```python
import jax, jax.numpy as jnp
from jax import lax
import numpy as np

D_MODEL = 1024
BATCH = 2
SEQ = 16384
DEPTH = 4

GRID_W = 64
CTX_LEN = 256
N_HEADS = 8
N_KV_HEADS = 2
HEAD_DIM = 64
GROUP = N_HEADS // N_KV_HEADS
WINDOW = 128
BLOCK = 128
ROPE_BASE = 10000.0
ROPE_FREQS = HEAD_DIM // 4
CONF_W = 256
CONF_K = 31
SC_W = 256
SC_K = 3
FN_W = 256
FN_GROUPS = 4
FN_GROUP_W = FN_W // FN_GROUPS
N_BRANCH = 4
Q_W = N_HEADS * HEAD_DIM
KV_W = N_KV_HEADS * HEAD_DIM
IN_SIZES = (Q_W, KV_W, KV_W, 2 * CONF_W, SC_W, SC_W, SC_W, FN_W, N_BRANCH * D_MODEL)
IN_W = Q_W + 2 * KV_W + 2 * CONF_W + 3 * SC_W + FN_W + N_BRANCH * D_MODEL
D_FF = 2816
FFN_K = 3
EPS = 1e-6
NEG_INF = -1e30

kernel_name = "hybrid_parallel_mixer_dit_block"


def split_cols(p, sizes):
    idx = []
    acc = 0
    for n in sizes[:-1]:
        acc += n
        idx.append(acc)
    return jnp.split(p, idx, axis=-1)


def rms_norm(x, g):
    xf = x.astype(jnp.float32)
    y = xf * lax.rsqrt(jnp.mean(xf * xf, axis=-1, keepdims=True) + EPS)
    return (y * g.astype(jnp.float32)).astype(x.dtype)


def layer_norm(x, g, b):
    xf = x.astype(jnp.float32)
    mu = jnp.mean(xf, axis=-1, keepdims=True)
    xc = xf - mu
    var = jnp.mean(xc * xc, axis=-1, keepdims=True)
    y = xc * lax.rsqrt(var + EPS) * g.astype(jnp.float32) + b.astype(jnp.float32)
    return y.astype(x.dtype)


def adaln(cvec, w, b):
    m = jax.nn.silu(cvec) @ w + b
    return [t[..., None, :] for t in jnp.split(m, 6, axis=-1)]


def modulate(h, shift, scale):
    return h * (1 + scale) + shift


def dwconv(x, w, b=None):
    k, ch = w.shape
    pad = (k - 1) // 2
    y = lax.conv_general_dilated(
        x, w[:, None, :].astype(x.dtype), window_strides=(1,), padding=[(pad, pad)],
        dimension_numbers=("NWC", "WIO", "NWC"), feature_group_count=ch)
    if b is not None:
        y = y + b.astype(x.dtype)
    return y


def axial_rope_tables(n_tokens):
    rows = n_tokens // GRID_W
    r = jnp.repeat(jnp.arange(rows), GRID_W)
    col = jnp.tile(jnp.arange(GRID_W), rows)
    inv = ROPE_BASE ** (-jnp.arange(ROPE_FREQS, dtype=jnp.float32) / ROPE_FREQS)
    ang = jnp.stack([r, col], axis=-1).astype(jnp.float32)[:, :, None] * inv
    return jnp.cos(ang)[:, None, :, None, :], jnp.sin(ang)[:, None, :, None, :]


def apply_rope(x, cos, sin):
    b, s, h, _ = x.shape
    xs = x.astype(jnp.float32).reshape(b, s, h, 2, 2, ROPE_FREQS)
    rot = jnp.stack([-xs[..., 1, :], xs[..., 0, :]], axis=-2)
    return (xs * cos + rot * sin).reshape(b, s, h, HEAD_DIM).astype(x.dtype)


def attend_with_sink(q, k, v, sink, mask):
    s = jnp.einsum("bqhgd,bkhd->bhgqk", q, k, preferred_element_type=jnp.float32) * (HEAD_DIM ** -0.5)
    if mask is not None:
        s = jnp.where(mask, s, NEG_INF)
    sk = sink.astype(jnp.float32)[None, :, :, None, None]
    m = jnp.maximum(jnp.max(s, axis=-1, keepdims=True), sk)
    p = jnp.exp(s - m)
    denom = jnp.sum(p, axis=-1, keepdims=True) + jnp.exp(sk - m)
    return jnp.einsum("bhgqk,bkhd->bqhgd", (p / denom).astype(v.dtype), v)


def windowed_latent_attention(q, k, v, kc, vc, sink):
    b, s = q.shape[:2]
    nb = s // BLOCK
    band = BLOCK + 2 * WINDOW

    def bands(t):
        tp = jnp.pad(t, ((0, 0), (WINDOW, WINDOW), (0, 0), (0, 0)))
        tb = tp.reshape(b, nb + 2, BLOCK, N_KV_HEADS, HEAD_DIM)
        tb = jnp.concatenate([tb[:, :-2], tb[:, 1:-1], tb[:, 2:]], axis=2)
        return jnp.moveaxis(tb, 1, 0)

    qb = jnp.moveaxis(q.reshape(b, nb, BLOCK, N_KV_HEADS, GROUP, HEAD_DIM), 1, 0)
    rel = jnp.arange(band)[None, :] - WINDOW - jnp.arange(BLOCK)[:, None]
    in_win = jnp.abs(rel) <= WINDOW
    kpos = jnp.arange(nb)[:, None] * BLOCK - WINDOW + jnp.arange(band)[None, :]
    valid = in_win[None] & ((kpos >= 0) & (kpos < s))[:, None, :]
    ctx_ok = jnp.ones((BLOCK, kc.shape[1]), dtype=bool)

    def one_block(args):
        qi, ki, vi, mi = args
        kk = jnp.concatenate([ki, kc], axis=1)
        vv = jnp.concatenate([vi, vc], axis=1)
        mm = jnp.concatenate([mi, ctx_ok], axis=1)
        return attend_with_sink(qi, kk, vv, sink, mm)

    out = lax.map(one_block, (qb, bands(k), bands(v), valid))
    return jnp.moveaxis(out, 0, 1).reshape(b, s, Q_W)


def local_branches(p_conf, p_b, p_c, p_x, p_fn, conf_dw_w, conf_dw_b, conf_ln_g, conf_ln_b, sc_w):
    a, g = jnp.split(p_conf, 2, axis=-1)
    u = dwconv(a * jax.nn.sigmoid(g), conf_dw_w, conf_dw_b)
    y_conf = jax.nn.silu(layer_norm(u, conf_ln_g, conf_ln_b))
    y_sc = p_b * dwconv(p_c * p_x, sc_w)
    bsz, n = p_fn.shape[:2]
    f = p_fn.astype(jnp.float32).reshape(bsz, n, FN_GROUPS, FN_GROUP_W)
    y_fn = jnp.fft.fft2(f, axes=(1, 3), norm="ortho").real.astype(p_fn.dtype).reshape(bsz, n, FN_W)
    return y_conf, y_sc, y_fn


def merge_branches(ys, p_gate, w_brs, w_out):
    gates = jnp.split(jax.nn.sigmoid(p_gate), N_BRANCH, axis=-1)
    m = gates[0] * (ys[0] @ w_brs[0])
    for g, y, w in zip(gates[1:], ys[1:], w_brs[1:]):
        m = m + g * (y @ w)
    return m @ w_out


def mixer_output(p, y_attn, conf_dw_w, conf_dw_b, conf_ln_g, conf_ln_b, sc_w, w_brs, w_out):
    y_conf, y_sc, y_fn = local_branches(p[3], p[4], p[5], p[6], p[7],
                                        conf_dw_w, conf_dw_b, conf_ln_g, conf_ln_b, sc_w)
    return merge_branches((y_attn, y_conf, y_sc, y_fn), p[8], w_brs, w_out)


def conv_ffn(h, w_up, dw_w, dw_b, w_down):
    a, b = jnp.split(h @ w_up, 2, axis=-1)
    return (jax.nn.silu(dwconv(a, dw_w, dw_b)) * b) @ w_down


def setup_inputs(seed: int = 0) -> dict:
    key = jax.random.key(seed)
    ks = jax.random.split(key, 25)
    D = D_MODEL

    def nrm(k, shape, s):
        return jax.random.normal(k, shape, jnp.float32) * s

    return {
        "x": nrm(ks[0], (BATCH, SEQ, D), 1.0),
        "c": nrm(ks[1], (BATCH, D), 1.0),
        "ctx": nrm(ks[2], (BATCH, CTX_LEN, D), 1.0),
        "c_ctx": nrm(ks[3], (D,), 1.0),
        "w_ada": nrm(ks[4], (DEPTH, D, 6 * D), 0.5 * D ** -0.5),
        "b_ada": nrm(ks[5], (DEPTH, 6 * D), 0.02),
        "g_mix": 1.0 + nrm(ks[6], (DEPTH, D), 0.02),
        "w_in": nrm(ks[7], (DEPTH, D, IN_W), D ** -0.5),
        "attn_sink": nrm(ks[8], (DEPTH, N_HEADS), 1.0),
        "conf_dw_w": nrm(ks[9], (DEPTH, CONF_K, CONF_W), CONF_K ** -0.5),
        "conf_dw_b": nrm(ks[10], (DEPTH, CONF_W), 0.02),
        "conf_ln_g": 1.0 + nrm(ks[11], (DEPTH, CONF_W), 0.02),
        "conf_ln_b": nrm(ks[12], (DEPTH, CONF_W), 0.02),
        "sc_w": nrm(ks[13], (DEPTH, SC_K, SC_W), SC_K ** -0.5),
        "w_br_attn": nrm(ks[14], (DEPTH, Q_W, D), Q_W ** -0.5),
        "w_br_conf": nrm(ks[15], (DEPTH, CONF_W, D), CONF_W ** -0.5),
        "w_br_sconv": nrm(ks[16], (DEPTH, SC_W, D), SC_W ** -0.5),
        "w_br_fourier": nrm(ks[17], (DEPTH, FN_W, D), FN_W ** -0.5),
        "w_out": nrm(ks[18], (DEPTH, D, D), D ** -0.5),
        "g_ffn": 1.0 + nrm(ks[19], (DEPTH, D), 0.02),
        "w_up": nrm(ks[20], (DEPTH, D, 2 * D_FF), D ** -0.5),
        "ffn_dw_w": nrm(ks[21], (DEPTH, FFN_K, D_FF), FFN_K ** -0.5),
        "ffn_dw_b": nrm(ks[22], (DEPTH, D_FF), 0.02),
        "w_down": nrm(ks[23], (DEPTH, D_FF, D), D_FF ** -0.5),
        "g_final": 1.0 + nrm(ks[24], (D,), 0.02),
    }


def reference(x, c, ctx, c_ctx, w_ada, b_ada, g_mix, w_in, attn_sink, conf_dw_w, conf_dw_b,
              conf_ln_g, conf_ln_b, sc_w, w_br_attn, w_br_conf, w_br_sconv, w_br_fourier, w_out,
              g_ffn, w_up, ffn_dw_w, ffn_dw_b, w_down, g_final):
    b, s = x.shape[:2]
    lc = ctx.shape[1]
    cos, sin = axial_rope_tables(s)
    for l in range(DEPTH):
        last = l == DEPTH - 1
        sh1, sc1, gt1, sh2, sc2, gt2 = adaln(c, w_ada[l], b_ada[l])
        csh1, csc1, cgt1, csh2, csc2, cgt2 = adaln(c_ctx, w_ada[l], b_ada[l])
        sink = attn_sink[l].reshape(N_KV_HEADS, GROUP)
        w_brs = (w_br_attn[l], w_br_conf[l], w_br_sconv[l], w_br_fourier[l])

        hx = modulate(rms_norm(x, g_mix[l]), sh1, sc1)
        hc = modulate(rms_norm(ctx, g_mix[l]), csh1, csc1)
        px = split_cols(hx @ w_in[l], IN_SIZES)
        q = apply_rope(px[0].reshape(b, s, N_HEADS, HEAD_DIM), cos, sin)
        k = apply_rope(px[1].reshape(b, s, N_KV_HEADS, HEAD_DIM), cos, sin)
        v = px[2].reshape(b, s, N_KV_HEADS, HEAD_DIM)
        if last:
            kc, vc = jnp.split(hc @ w_in[l][:, Q_W:Q_W + 2 * KV_W], 2, axis=-1)
        else:
            pc = split_cols(hc @ w_in[l], IN_SIZES)
            kc, vc = pc[1], pc[2]
        kc = kc.reshape(b, lc, N_KV_HEADS, HEAD_DIM)
        vc = vc.reshape(b, lc, N_KV_HEADS, HEAD_DIM)

        y_attn = windowed_latent_attention(q, k, v, kc, vc, sink)
        x = x + gt1 * mixer_output(px, y_attn, conf_dw_w[l], conf_dw_b[l], conf_ln_g[l],
                                   conf_ln_b[l], sc_w[l], w_brs, w_out[l])
        if not last:
            qc = pc[0].reshape(b, lc, N_KV_HEADS, GROUP, HEAD_DIM)
            yc_attn = attend_with_sink(qc, kc, vc, sink, None).reshape(b, lc, Q_W)
            ctx = ctx + cgt1 * mixer_output(pc, yc_attn, conf_dw_w[l], conf_dw_b[l], conf_ln_g[l],
                                            conf_ln_b[l], sc_w[l], w_brs, w_out[l])

        x = x + gt2 * conv_ffn(modulate(rms_norm(x, g_ffn[l]), sh2, sc2),
                               w_up[l], ffn_dw_w[l], ffn_dw_b[l], w_down[l])
        if not last:
            ctx = ctx + cgt2 * conv_ffn(modulate(rms_norm(ctx, g_ffn[l]), csh2, csc2),
                                        w_up[l], ffn_dw_w[l], ffn_dw_b[l], w_down[l])
    return rms_norm(x, g_final)
```

```python
import functools

import numpy as np
import jax
import jax.numpy as jnp
from jax import lax
from jax.experimental import pallas as pl
from jax.experimental.pallas import tpu as pltpu

F32 = jnp.float32
BF16 = jnp.bfloat16

D_MODEL = 1024
DEPTH = 4
GRID_W = 64
N_HEADS = 8
N_KV_HEADS = 2
HEAD_DIM = 64
WINDOW = 128
BLOCK = 128
ROPE_BASE = 10000.0
ROPE_FREQS = HEAD_DIM // 4
CONF_W = 256
CONF_K = 31
SC_W = 256
SC_K = 3
FN_W = 256
FN_GROUP_W = 64
N_BRANCH = 4
Q_W = N_HEADS * HEAD_DIM
KV_W = N_KV_HEADS * HEAD_DIM
MAIN_W = Q_W + 2 * KV_W + 2 * CONF_W + 3 * SC_W + FN_W
D_FF = 2816
EPS = 1e-6
NEG_INF = -1e30

LANES = 128
SUBLANES = 8
BF16_ROWS = 16
VMEM_LIMIT = 56 * 1024 * 1024
FFN_CHUNK = 256


def _cparams(*sem):
    return pltpu.CompilerParams(dimension_semantics=sem, vmem_limit_bytes=VMEM_LIMIT)


def _const_spec(shape):
    nd = len(shape)
    return pl.BlockSpec(shape, lambda *_: (0,) * nd, pipeline_mode=pl.Buffered(1))


def _sigmoid(x):
    return 1.0 / (1.0 + jnp.exp(-x))


def _norm_mod(x, g, shift, scale):
    ms = jnp.mean(x * x, axis=-1, keepdims=True)
    y = x * lax.rsqrt(ms + EPS) * g
    return y * (1.0 + scale) + shift


def _dot(a, b):
    return jnp.dot(a, b, preferred_element_type=F32)


def _ada_kernel(c_ref, w_ref, b_ref, o_ref):
    c = c_ref[...]
    a = (c * _sigmoid(c)).astype(BF16)
    o_ref[0] = _dot(a, w_ref[0].astype(BF16)) + b_ref[0]


def _ada_call(cvec, w_ada, b_ada):
    depth, d, n = w_ada.shape
    tn = 1536
    return pl.pallas_call(
        _ada_kernel,
        grid=(depth, n // tn),
        in_specs=[
            pl.BlockSpec((SUBLANES, d), lambda l, j: (0, 0)),
            pl.BlockSpec((1, d, tn), lambda l, j: (l, 0, j)),
            pl.BlockSpec((1, 1, tn), lambda l, j: (l, 0, j)),
        ],
        out_specs=pl.BlockSpec((1, SUBLANES, tn), lambda l, j: (l, 0, j)),
        out_shape=jax.ShapeDtypeStruct((depth, SUBLANES, n), F32),
        compiler_params=_cparams("parallel", "parallel"),
        name="adaln",
    )(cvec, w_ada, b_ada.reshape(depth, 1, n))


def _dup_halves(t):
    r = pltpu.roll(t, 64, axis=1)
    lo = lax.broadcasted_iota(jnp.int32, t.shape, 1) < 64
    return jnp.concatenate([jnp.where(lo, t, r), jnp.where(lo, r, t)], axis=1)


def _rope(t, cos, sin_a, sin_b):
    out = []
    for j in range(t.shape[1] // LANES):
        blk = t[:, j * LANES:(j + 1) * LANES]
        out.append(blk * cos + pltpu.roll(blk, LANES - 16, axis=1) * sin_a
                   + pltpu.roll(blk, 16, axis=1) * sin_b)
    return jnp.concatenate(out, axis=1) if len(out) > 1 else out[0]


def _proj_kernel(*refs, rope, kv_only):
    if rope:
        x_ref, sh_ref, sc_ref, g_ref, w_ref, cos_ref, sa_ref, sb_ref = refs[:8]
        outs = refs[8:]
    else:
        x_ref, sh_ref, sc_ref, g_ref, w_ref = refs[:5]
        outs = refs[5:]
    hb = _norm_mod(x_ref[0], g_ref[...], sh_ref[0], sc_ref[0]).astype(BF16)

    if kv_only:
        k_ref, v_ref = outs
        kv = _dot(hb, w_ref[:, Q_W:Q_W + 2 * KV_W])
        k_ref[0] = _dup_halves(kv[:, :KV_W]).astype(BF16)
        v_ref[0] = _dup_halves(kv[:, KV_W:]).astype(BF16)
        return

    q_ref, k_ref, v_ref, u0_ref, pb_ref, cx_ref, fn_ref = outs
    qkv = _dot(hb, w_ref[:, 0:Q_W + 2 * KV_W])
    q = qkv[:, :Q_W]
    k = qkv[:, Q_W:Q_W + KV_W]
    v = qkv[:, Q_W + KV_W:]
    if rope:
        cos, sa, sb = cos_ref[...], sa_ref[...], sb_ref[...]
        q = _rope(q, cos, sa, sb)
        k = _rope(k, cos, sa, sb)
    q_ref[0] = (q * (HEAD_DIM ** -0.5)).astype(BF16)
    k_ref[0] = _dup_halves(k).astype(BF16)
    v_ref[0] = _dup_halves(v).astype(BF16)

    c0 = Q_W + 2 * KV_W
    conf = _dot(hb, w_ref[:, c0:c0 + 2 * CONF_W])
    u0_ref[0] = (conf[:, :CONF_W] * _sigmoid(conf[:, CONF_W:])).astype(BF16)
    c0 += 2 * CONF_W
    pb_ref[0] = _dot(hb, w_ref[:, c0:c0 + SC_W]).astype(BF16)
    c0 += SC_W
    pcx = _dot(hb, w_ref[:, c0:c0 + 2 * SC_W])
    cx_ref[0] = (pcx[:, :SC_W] * pcx[:, SC_W:]).astype(BF16)
    c0 += 2 * SC_W
    fn_ref[0] = _dot(hb, w_ref[:, c0:c0 + FN_W])


def _proj_call(x, shift, scale, g, w_main, rope_tabs, *, tm, kv_only=False):
    b, s, d = x.shape
    nt = s // tm
    rope = rope_tabs is not None
    mod_spec = pl.BlockSpec((1, 1, d), (lambda bi, i: (bi, 0, 0)) if shift.shape[0] > 1
                            else (lambda bi, i: (0, 0, 0)))
    in_specs = [
        pl.BlockSpec((1, tm, d), lambda bi, i: (bi, i, 0)),
        mod_spec, mod_spec,
        _const_spec((1, d)),
        _const_spec(w_main.shape),
    ]
    args = [x, shift, scale, g, w_main]
    if rope:
        in_specs += [pl.BlockSpec((tm, LANES), lambda bi, i: (i, 0))] * 3
        args += list(rope_tabs)

    def out(width, dtype):
        return (pl.BlockSpec((1, tm, width), lambda bi, i: (bi, i, 0)),
                jax.ShapeDtypeStruct((b, s, width), dtype))

    if kv_only:
        outs = [out(2 * KV_W, BF16), out(2 * KV_W, BF16)]
    else:
        outs = [out(Q_W, BF16), out(2 * KV_W, BF16), out(2 * KV_W, BF16), out(CONF_W, BF16),
                out(SC_W, BF16), out(SC_W, BF16), out(FN_W, F32)]
    return pl.pallas_call(
        functools.partial(_proj_kernel, rope=rope, kv_only=kv_only),
        grid=(b, nt),
        in_specs=in_specs,
        out_specs=[o[0] for o in outs],
        out_shape=[o[1] for o in outs],
        compiler_params=_cparams("parallel", "parallel"),
        name="proj_kv" if kv_only else ("proj_rope" if rope else "proj"),
    )(*args)


def _attend(q, kd, vd, mask, sink_ref):
    rows = q.shape[0]
    group = N_HEADS // N_KV_HEADS
    lo = lax.broadcasted_iota(jnp.int32, (rows, LANES), 1) < 64
    qf = q.astype(F32)
    out_blocks = []
    for hk in range(N_KV_HEADS):
        kh = kd[:, hk * LANES:(hk + 1) * LANES]
        vh = vd[:, hk * LANES:(hk + 1) * LANES]
        qs, sks = [], []
        for p in range(group // 2):
            blk = qf[:, (2 * hk + p) * LANES:(2 * hk + p + 1) * LANES]
            qs.append(jnp.where(lo, blk, 0.0))
            qs.append(jnp.where(lo, 0.0, blk))
        for g in range(group):
            hd = group * hk + g
            sks.append(jnp.broadcast_to(sink_ref[hd:hd + 1, 0:1], (rows, 1)))
        qst = jnp.concatenate(qs, axis=0).astype(BF16)
        sk = jnp.concatenate(sks, axis=0)
        s = lax.dot_general(qst, kh, (((1,), (1,)), ((), ())), preferred_element_type=F32)
        if mask is not None:
            s = jnp.where(mask, s, NEG_INF)
        m = jnp.maximum(jnp.max(s, axis=-1, keepdims=True), sk)
        p_ = jnp.exp(s - m)
        denom = jnp.sum(p_, axis=-1, keepdims=True) + jnp.exp(sk - m)
        o = _dot(p_.astype(BF16), vh) / denom
        out_blocks.append(jnp.where(lo, o[0:rows], o[rows:2 * rows]))
        out_blocks.append(jnp.where(lo, o[2 * rows:3 * rows], o[3 * rows:4 * rows]))
    return jnp.concatenate(out_blocks, axis=1)


def _attn_kernel(q_ref, kp_ref, kc_ref, kn_ref, vp_ref, vc_ref, vn_ref, kx_ref, vx_ref, sink_ref,
                 o_ref, *, nb):
    i = pl.program_id(1)
    lc = kx_ref.shape[1]
    group = N_HEADS // N_KV_HEADS
    shape = (group * BLOCK, 3 * BLOCK + lc)
    row = lax.broadcasted_iota(jnp.int32, shape, 0) & (BLOCK - 1)
    col = lax.broadcasted_iota(jnp.int32, shape, 1)
    off_p = jnp.where(i > 0, 0, BLOCK)
    off_n = jnp.where(i < nb - 1, 0, BLOCK)
    ok_prev = col >= row + off_p
    ok_next = col - 2 * BLOCK <= row - off_n
    is_prev = col < BLOCK
    is_next = jnp.logical_and(col >= 2 * BLOCK, col < 3 * BLOCK)
    mask = jnp.logical_or(
        jnp.logical_or(jnp.logical_and(is_prev, ok_prev), jnp.logical_and(is_next, ok_next)),
        jnp.logical_not(jnp.logical_or(is_prev, is_next)))
    kd = jnp.concatenate([kp_ref[0], kc_ref[0], kn_ref[0], kx_ref[0]], axis=0)
    vd = jnp.concatenate([vp_ref[0], vc_ref[0], vn_ref[0], vx_ref[0]], axis=0)
    o_ref[0] = _attend(q_ref[0], kd, vd, mask, sink_ref).astype(BF16)


def _attn_call(q, k, v, kx, vx, sink_b):
    b, s, _ = q.shape
    nb = s // BLOCK
    lc = kx.shape[1]
    kvw = 2 * KV_W

    def kv_spec(off):
        def idx(bi, i):
            return (bi, jnp.clip(i + off, 0, nb - 1), 0)
        return pl.BlockSpec((1, BLOCK, kvw), idx)

    ctx_spec = pl.BlockSpec((1, lc, kvw), lambda bi, i: (bi, 0, 0))
    return pl.pallas_call(
        functools.partial(_attn_kernel, nb=nb),
        grid=(b, nb),
        in_specs=[pl.BlockSpec((1, BLOCK, Q_W), lambda bi, i: (bi, i, 0)),
                  kv_spec(-1), kv_spec(0), kv_spec(1), kv_spec(-1), kv_spec(0), kv_spec(1),
                  ctx_spec, ctx_spec, _const_spec(sink_b.shape)],
        out_specs=pl.BlockSpec((1, BLOCK, Q_W), lambda bi, i: (bi, i, 0)),
        out_shape=jax.ShapeDtypeStruct((b, s, Q_W), BF16),
        compiler_params=_cparams("parallel", "parallel"),
        name="attn_window",
    )(q, k, k, k, v, v, v, kx, vx, sink_b)


def _attn_ctx_kernel(q_ref, k_ref, v_ref, sink_ref, o_ref):
    o_ref[0] = _attend(q_ref[0], k_ref[0], v_ref[0], None, sink_ref).astype(BF16)


def _attn_ctx_call(q, k, v, sink_b):
    b, lc, _ = q.shape
    kvw = 2 * KV_W
    return pl.pallas_call(
        _attn_ctx_kernel,
        grid=(b,),
        in_specs=[pl.BlockSpec((1, lc, Q_W), lambda bi: (bi, 0, 0)),
                  pl.BlockSpec((1, lc, kvw), lambda bi: (bi, 0, 0)),
                  pl.BlockSpec((1, lc, kvw), lambda bi: (bi, 0, 0)),
                  _const_spec(sink_b.shape)],
        out_specs=pl.BlockSpec((1, lc, Q_W), lambda bi: (bi, 0, 0)),
        out_shape=jax.ShapeDtypeStruct((b, lc, Q_W), BF16),
        compiler_params=_cparams("parallel"),
        name="attn_ctx",
    )(q, k, v, sink_b)


def _dft_consts(n_pos):
    r = int(round(np.sqrt(n_pos)))
    assert n_pos == r * r and r % SUBLANES == 0
    idx = np.arange(r)
    ang = 2.0 * np.pi * np.outer(idx, idx) / r
    c, s = np.cos(ang), np.sin(ang)
    eye = np.eye(SUBLANES)
    m1 = np.concatenate([np.kron(c, eye), np.kron(-s, eye)], axis=0)
    def perm_kron(f):
        return np.einsum("ks,lm->klms", f, eye).reshape(r * SUBLANES, SUBLANES * r)
    m2 = np.block([[perm_kron(c), perm_kron(s)], [perm_kron(-s), perm_kron(c)]])
    sfull = np.arange(r).reshape(r // SUBLANES, 1, SUBLANES)
    tw = 2.0 * np.pi * idx.reshape(1, r, 1) * sfull / n_pos
    tw = tw.reshape(r // SUBLANES, r * SUBLANES, 1)
    twc = np.broadcast_to(np.cos(tw), tw.shape[:2] + (LANES,))
    tws = np.broadcast_to(np.sin(tw), tw.shape[:2] + (LANES,))
    return (m1.astype(np.float32), m2.astype(np.float32),
            np.ascontiguousarray(twc, np.float32), np.ascontiguousarray(tws, np.float32))


def _chan_consts():
    idx = np.arange(FN_GROUP_W)
    ang = 2.0 * np.pi * np.outer(idx, idx) / FN_GROUP_W
    groups = FN_W // FN_GROUP_W
    cc = np.kron(np.eye(groups), np.cos(ang))
    sc = np.kron(np.eye(groups), np.sin(ang))
    return np.concatenate([cc, sc], axis=0).astype(np.float32)


def _fft1_kernel(x_ref, m1_ref, twc_ref, tws_ref, o_ref):
    r = x_ref.shape[1]
    x = x_ref[0].reshape(r * SUBLANES, FN_W).astype(BF16)
    t = _dot(m1_ref[...], x)
    half = r * SUBLANES
    tr, ti = t[:half], t[half:]
    cw = jnp.concatenate([twc_ref[0]] * (FN_W // LANES), axis=1)
    sw = jnp.concatenate([tws_ref[0]] * (FN_W // LANES), axis=1)
    o_ref[0, 0] = (tr * cw + ti * sw).reshape(r, SUBLANES, FN_W)
    o_ref[0, 1] = (ti * cw - tr * sw).reshape(r, SUBLANES, FN_W)


def _fft2_kernel(t_ref, m2_ref, ch_ref, o_ref, *, scale):
    r = t_ref.shape[3]
    half = r * SUBLANES
    tr = t_ref[0, 0].reshape(half, FN_W)
    ti = t_ref[0, 1].reshape(half, FN_W)
    t = jnp.concatenate([tr, ti], axis=0).astype(BF16)
    z = _dot(m2_ref[...], t)
    zc = jnp.concatenate([z[:half], z[half:]], axis=1).astype(BF16)
    y = _dot(zc, ch_ref[...]) * scale
    o_ref[0] = y.reshape(r, SUBLANES, FN_W)


def _fft_call(fn, consts):
    m1, m2, twc, tws, ch = consts
    b, n, w = fn.shape
    r = m1.shape[1] // SUBLANES
    nblk = r // SUBLANES
    x4 = fn.reshape(b, r, r, w)
    t = pl.pallas_call(
        _fft1_kernel,
        grid=(b, nblk),
        in_specs=[pl.BlockSpec((1, r, SUBLANES, w), lambda bi, j: (bi, 0, j, 0)),
                  _const_spec(m1.shape),
                  pl.BlockSpec((1, r * SUBLANES, LANES), lambda bi, j: (j, 0, 0)),
                  pl.BlockSpec((1, r * SUBLANES, LANES), lambda bi, j: (j, 0, 0))],
        out_specs=pl.BlockSpec((1, 2, r, SUBLANES, w), lambda bi, j: (bi, 0, 0, j, 0)),
        out_shape=jax.ShapeDtypeStruct((b, 2, r, r, w), F32),
        compiler_params=_cparams("parallel", "parallel"),
        name="fft_stage1",
    )(x4, m1, twc, tws)
    y = pl.pallas_call(
        functools.partial(_fft2_kernel, scale=float(1.0 / np.sqrt(n * FN_GROUP_W))),
        grid=(b, nblk),
        in_specs=[pl.BlockSpec((1, 2, SUBLANES, r, w), lambda bi, j: (bi, 0, j, 0, 0)),
                  _const_spec(m2.shape),
                  _const_spec(ch.shape)],
        out_specs=pl.BlockSpec((1, r, SUBLANES, w), lambda bi, j: (bi, 0, j, 0)),
        out_shape=jax.ShapeDtypeStruct((b, r, r, w), F32),
        compiler_params=_cparams("parallel", "parallel"),
        name="fft_stage2",
    )(t, m2, ch)
    return y.reshape(b, n, w)


def _fft_small_kernel(x_ref, m_ref, ch_ref, o_ref, *, scale):
    n = x_ref.shape[1]
    z = _dot(m_ref[...], x_ref[0].astype(BF16))
    zc = jnp.concatenate([z[:n], z[n:]], axis=1).astype(BF16)
    o_ref[0] = _dot(zc, ch_ref[...]) * scale


def _fft_small_call(fn, m, ch):
    b, n, w = fn.shape
    return pl.pallas_call(
        functools.partial(_fft_small_kernel, scale=float(1.0 / np.sqrt(n * FN_GROUP_W))),
        grid=(b,),
        in_specs=[pl.BlockSpec((1, n, w), lambda bi: (bi, 0, 0)),
                  _const_spec(m.shape), _const_spec(ch.shape)],
        out_specs=pl.BlockSpec((1, n, w), lambda bi: (bi, 0, 0)),
        out_shape=jax.ShapeDtypeStruct((b, n, w), F32),
        compiler_params=_cparams("parallel"),
        name="fft_ctx",
    )(fn, m, ch)


def _small_dft_const(n):
    idx = np.arange(n)
    ang = 2.0 * np.pi * np.outer(idx, idx) / n
    return np.concatenate([np.cos(ang), -np.sin(ang)], axis=0).astype(np.float32)


HALO = 16


def _merge_kernel(x_ref, sh_ref, sc_ref, gt_ref, g_ref, ya_ref,
                  u0p_ref, u0_ref, u0n_ref, cxp_ref, cx_ref, cxn_ref, pb_ref, yf_ref,
                  wg_ref, wba_ref, wbc_ref, wbs_ref, wbf_ref, wo_ref,
                  cw_ref, cb_ref, lg_ref, lb_ref, scw_ref,
                  o_ref, uext_ref, cext_ref, *, tm, nt):
    i = pl.program_id(1)
    first = i == 0
    last = i == nt - 1
    x = x_ref[0]
    hb = _norm_mod(x, g_ref[...], sh_ref[0], sc_ref[0]).astype(BF16)

    def fill(ext_ref, p_ref, c_ref, n_ref):
        ext_ref[0:HALO] = jnp.where(first, 0.0, p_ref[0].astype(F32))
        ext_ref[HALO:HALO + tm] = c_ref[0].astype(F32)
        ext_ref[HALO + tm:2 * HALO + tm] = jnp.where(last, 0.0, n_ref[0].astype(F32))

    fill(uext_ref, u0p_ref, u0_ref, u0n_ref)
    fill(cext_ref, cxp_ref, cx_ref, cxn_ref)

    pad = (CONF_K - 1) // 2
    acc = jnp.zeros((tm, CONF_W), F32) + cb_ref[...]
    for j in range(CONF_K):
        acc = acc + cw_ref[j:j + 1, :] * uext_ref[pl.ds(HALO - pad + j, tm), :]
    mu = jnp.mean(acc, axis=-1, keepdims=True)
    xc = acc - mu
    var = jnp.mean(xc * xc, axis=-1, keepdims=True)
    yn = xc * lax.rsqrt(var + EPS) * lg_ref[...] + lb_ref[...]
    y_conf = yn * _sigmoid(yn)

    pad = (SC_K - 1) // 2
    conv = jnp.zeros((tm, SC_W), F32)
    for j in range(SC_K):
        conv = conv + scw_ref[j:j + 1, :] * cext_ref[pl.ds(HALO - pad + j, tm), :]
    y_sc = pb_ref[0].astype(F32) * conv

    ys = (ya_ref[0], y_conf.astype(BF16), y_sc.astype(BF16), yf_ref[0].astype(BF16))
    wbs = (wba_ref, wbc_ref, wbs_ref, wbf_ref)
    m = jnp.zeros((tm, D_MODEL), F32)
    for idx in range(N_BRANCH):
        gate = _sigmoid(_dot(hb, wg_ref[:, idx * D_MODEL:(idx + 1) * D_MODEL]))
        m = m + gate * _dot(ys[idx], wbs[idx][...])
    o_ref[0] = x + gt_ref[0] * _dot(m.astype(BF16), wo_ref[...])


def _merge_call(x, mods, g, ya, u0, cx, pb, yf, weights, *, tm):
    b, s, d = x.shape
    nt = s // tm
    shift, scale, gate = mods
    wg, wba, wbc, wbs, wbf, wo, cw, cb, lg, lb, scw = weights
    per_batch = shift.shape[0] > 1
    mod_spec = pl.BlockSpec((1, 1, d), (lambda bi, i: (bi, 0, 0)) if per_batch
                            else (lambda bi, i: (0, 0, 0)))
    hpt = tm // HALO
    nh = s // HALO

    def tile(width):
        return pl.BlockSpec((1, tm, width), lambda bi, i: (bi, i, 0))

    def halo(width, off):
        if off < 0:
            return pl.BlockSpec((1, HALO, width),
                                lambda bi, i: (bi, jnp.maximum(i * hpt - 1, 0), 0))
        return pl.BlockSpec((1, HALO, width),
                            lambda bi, i: (bi, jnp.minimum((i + 1) * hpt, nh - 1), 0))

    in_specs = [tile(d), mod_spec, mod_spec, mod_spec, _const_spec((1, d)), tile(Q_W),
                halo(CONF_W, -1), tile(CONF_W), halo(CONF_W, 1),
                halo(SC_W, -1), tile(SC_W), halo(SC_W, 1), tile(SC_W), tile(FN_W)]
    in_specs += [_const_spec(w.shape) for w in weights]
    return pl.pallas_call(
        functools.partial(_merge_kernel, tm=tm, nt=nt),
        grid=(b, nt),
        in_specs=in_specs,
        out_specs=tile(d),
        out_shape=jax.ShapeDtypeStruct((b, s, d), F32),
        scratch_shapes=[pltpu.VMEM((tm + 2 * HALO, CONF_W), F32),
                        pltpu.VMEM((tm + 2 * HALO, SC_W), F32)],
        compiler_params=_cparams("parallel", "parallel"),
        name="merge",
    )(x, shift, scale, gate, g, ya, u0, u0, u0, cx, cx, cx, pb, yf, *weights)


def _ffn_kernel(*refs, tm, nt, final):
    if final:
        (xp_ref, x_ref, xn_ref, sh_ref, sc_ref, gt_ref, g_ref, wu_ref, wd_ref, dw_ref, db_ref,
         gf_ref, o_ref, aext_ref) = refs
    else:
        (xp_ref, x_ref, xn_ref, sh_ref, sc_ref, gt_ref, g_ref, wu_ref, wd_ref, dw_ref, db_ref,
         o_ref, aext_ref) = refs
    i = pl.program_id(1)
    x = x_ref[0]
    xe = jnp.concatenate([xp_ref[0], x, xn_ref[0]], axis=0)
    he = _norm_mod(xe, g_ref[...], sh_ref[0], sc_ref[0]).astype(BF16)
    rid = lax.broadcasted_iota(jnp.int32, (tm + 2 * SUBLANES, 1), 0)
    lo = jnp.where(i > 0, 0, SUBLANES)
    hi = jnp.where(i < nt - 1, tm + 2 * SUBLANES, tm + SUBLANES)
    valid = jnp.logical_and(rid >= lo, rid < hi)

    acc = jnp.zeros((tm, D_MODEL), F32)
    for c in range(0, D_FF, FFN_CHUNK):
        aext_ref[...] = jnp.where(valid, _dot(he, wu_ref[:, c:c + FFN_CHUNK]), 0.0)
        conv = jnp.zeros((tm, FFN_CHUNK), F32) + db_ref[:, c:c + FFN_CHUNK]
        for j in range(3):
            conv = conv + dw_ref[j:j + 1, c:c + FFN_CHUNK] * aext_ref[pl.ds(SUBLANES - 1 + j, tm), :]
        bb = _dot(he, wu_ref[:, D_FF + c:D_FF + c + FFN_CHUNK])[SUBLANES:SUBLANES + tm]
        gch = (conv * _sigmoid(conv) * bb).astype(BF16)
        acc = acc + _dot(gch, wd_ref[c:c + FFN_CHUNK, :])
    out = x + gt_ref[0] * acc
    if final:
        ms = jnp.mean(out * out, axis=-1, keepdims=True)
        out = out * lax.rsqrt(ms + EPS) * gf_ref[...]
    o_ref[0] = out


def _ffn_call(x, mods, g, wu, wd, dw, db, g_final, *, tm):
    b, s, d = x.shape
    nt = s // tm
    shift, scale, gate = mods
    per_batch = shift.shape[0] > 1
    mod_spec = pl.BlockSpec((1, 1, d), (lambda bi, i: (bi, 0, 0)) if per_batch
                            else (lambda bi, i: (0, 0, 0)))
    hpt = tm // SUBLANES
    nh = s // SUBLANES
    final = g_final is not None
    in_specs = [
        pl.BlockSpec((1, SUBLANES, d), lambda bi, i: (bi, jnp.maximum(i * hpt - 1, 0), 0)),
        pl.BlockSpec((1, tm, d), lambda bi, i: (bi, i, 0)),
        pl.BlockSpec((1, SUBLANES, d), lambda bi, i: (bi, jnp.minimum((i + 1) * hpt, nh - 1), 0)),
        mod_spec, mod_spec, mod_spec, _const_spec((1, d)),
        _const_spec(wu.shape), _const_spec(wd.shape), _const_spec(dw.shape), _const_spec(db.shape),
    ]
    args = [x, x, x, shift, scale, gate, g, wu, wd, dw, db]
    if final:
        in_specs.append(_const_spec((1, d)))
        args.append(g_final)
    return pl.pallas_call(
        functools.partial(_ffn_kernel, tm=tm, nt=nt, final=final),
        grid=(b, nt),
        in_specs=in_specs,
        out_specs=pl.BlockSpec((1, tm, d), lambda bi, i: (bi, i, 0)),
        out_shape=jax.ShapeDtypeStruct((b, s, d), F32),
        scratch_shapes=[pltpu.VMEM((tm + 2 * SUBLANES, FFN_CHUNK), F32)],
        compiler_params=_cparams("parallel", "parallel"),
        name="ffn_final" if final else "ffn",
    )(*args)


def _rope_tables(s):
    n = np.arange(s)
    pos = np.stack([n // GRID_W, n % GRID_W], axis=0).astype(np.float64)
    inv = ROPE_BASE ** (-np.arange(ROPE_FREQS, dtype=np.float64) / ROPE_FREQS)
    lane = np.arange(LANES)
    axis = (lane % HEAD_DIM) // (2 * ROPE_FREQS)
    ang = pos[axis].T * inv[lane % ROPE_FREQS][None, :]
    first = (lane % (2 * ROPE_FREQS)) < ROPE_FREQS
    sin = np.sin(ang)
    return (np.cos(ang).astype(np.float32),
            np.where(first[None, :], -sin, 0.0).astype(np.float32),
            np.where(first[None, :], 0.0, sin).astype(np.float32))


def _pad_rows(w, rows):
    return jnp.pad(w, ((0, rows - w.shape[0]), (0, 0)))


def kernel(x, c, ctx, c_ctx, w_ada, b_ada, g_mix, w_in, attn_sink, conf_dw_w, conf_dw_b, conf_ln_g,
           conf_ln_b, sc_w, w_br_attn, w_br_conf, w_br_sconv, w_br_fourier, w_out, g_ffn, w_up,
           ffn_dw_w, ffn_dw_b, w_down, g_final):
    b, s, d = x.shape
    lc = ctx.shape[1]
    tm = 512

    cvec = jnp.concatenate([c, c_ctx[None, :], jnp.zeros((SUBLANES - b - 1, d), F32)], axis=0)
    mods = _ada_call(cvec, w_ada, b_ada)

    rope_tabs = tuple(jnp.asarray(t) for t in _rope_tables(s))
    m1, m2, twc, tws = _dft_consts(s)
    ch = jnp.asarray(_chan_consts()).astype(BF16)
    fft_consts = (jnp.asarray(m1).astype(BF16), jnp.asarray(m2).astype(BF16),
                  jnp.asarray(twc), jnp.asarray(tws), ch)
    m_ctx = jnp.asarray(_small_dft_const(lc)).astype(BF16)

    for l in range(DEPTH):
        last = l == DEPTH - 1
        ml = mods[l].reshape(SUBLANES, 6, d)
        lat = [ml[:b, k][:, None, :] for k in range(6)]
        cxm = [ml[b:b + 1, k][:, None, :] for k in range(6)]
        w_in_l = w_in[l].astype(BF16)
        w_main, w_gate = w_in_l[:, :MAIN_W], w_in_l[:, MAIN_W:]
        g1 = g_mix[l][None, :]
        g2 = g_ffn[l][None, :]
        sink_b = jnp.broadcast_to(attn_sink[l][:, None], (N_HEADS, LANES))
        merge_w = (w_gate, w_br_attn[l].astype(BF16), w_br_conf[l].astype(BF16),
                   w_br_sconv[l].astype(BF16), w_br_fourier[l].astype(BF16), w_out[l].astype(BF16),
                   _pad_rows(conf_dw_w[l], 32), conf_dw_b[l][None, :], conf_ln_g[l][None, :],
                   conf_ln_b[l][None, :], _pad_rows(sc_w[l], SUBLANES))
        wu, wd = w_up[l].astype(BF16), w_down[l].astype(BF16)
        dw, db = _pad_rows(ffn_dw_w[l], SUBLANES), ffn_dw_b[l][None, :]

        q, k, v, u0, pb, cx, fn = _proj_call(x, lat[0], lat[1], g1, w_main, rope_tabs, tm=tm)
        if last:
            kc, vc = _proj_call(ctx, cxm[0], cxm[1], g1, w_main, None, tm=lc, kv_only=True)
        else:
            qc, kc, vc, u0c, pbc, cxc, fnc = _proj_call(ctx, cxm[0], cxm[1], g1, w_main, None, tm=lc)

        ya = _attn_call(q, k, v, kc, vc, sink_b)
        yf = _fft_call(fn, fft_consts)
        x = _merge_call(x, (lat[0], lat[1], lat[2]), g1, ya, u0, cx, pb, yf, merge_w, tm=tm)
        if not last:
            yac = _attn_ctx_call(qc, kc, vc, sink_b)
            yfc = _fft_small_call(fnc, m_ctx, ch)
            ctx = _merge_call(ctx, (cxm[0], cxm[1], cxm[2]), g1, yac, u0c, cxc, pbc, yfc, merge_w, tm=lc)

        x = _ffn_call(x, (lat[3], lat[4], lat[5]), g2, wu, wd, dw, db,
                      g_final[None, :] if last else None, tm=tm)
        if not last:
            ctx = _ffn_call(ctx, (cxm[3], cxm[4], cxm[5]), g2, wu, wd, dw, db, None, tm=lc)
    return x
```

```python
import functools

import numpy as np
import jax
import jax.numpy as jnp
from jax import lax
from jax.experimental import pallas as pl
from jax.experimental.pallas import tpu as pltpu

F32 = jnp.float32
BF16 = jnp.bfloat16

D_MODEL = 1024
DEPTH = 4
GRID_W = 64
N_HEADS = 8
N_KV_HEADS = 2
HEAD_DIM = 64
WINDOW = 128
BLOCK = 128
ROPE_BASE = 10000.0
ROPE_FREQS = HEAD_DIM // 4
CONF_W = 256
CONF_K = 31
SC_W = 256
SC_K = 3
FN_W = 256
FN_GROUP_W = 64
N_BRANCH = 4
Q_W = N_HEADS * HEAD_DIM
KV_W = N_KV_HEADS * HEAD_DIM
MAIN_W = Q_W + 2 * KV_W + 2 * CONF_W + 3 * SC_W + FN_W
D_FF = 2816
EPS = 1e-6
NEG_INF = -1e30

LANES = 128
SUBLANES = 8
BF16_ROWS = 16
VMEM_LIMIT = 56 * 1024 * 1024
FFN_CHUNK = 256
MERGE_CHUNK = 256
ATTN_QB = 2


def _cparams(*sem):
    return pltpu.CompilerParams(dimension_semantics=sem, vmem_limit_bytes=VMEM_LIMIT)


def _const_spec(shape):
    nd = len(shape)
    return pl.BlockSpec(shape, lambda *_: (0,) * nd, pipeline_mode=pl.Buffered(1))


def _sigmoid(x):
    return 1.0 / (1.0 + jnp.exp(-x))


def _norm_mod(x, g, shift, scale):
    ms = jnp.mean(x * x, axis=-1, keepdims=True)
    y = x * lax.rsqrt(ms + EPS) * g
    return y * (1.0 + scale) + shift


def _dot(a, b):
    return jnp.dot(a, b, preferred_element_type=F32)


def _ada_kernel(c_ref, w_ref, b_ref, o_ref):
    c = c_ref[...]
    a = (c * _sigmoid(c)).astype(BF16)
    o_ref[0] = _dot(a, w_ref[0].astype(BF16)) + b_ref[0]


def _ada_call(cvec, w_ada, b_ada):
    depth, d, n = w_ada.shape
    tn = 1536
    return pl.pallas_call(
        _ada_kernel,
        grid=(depth, n // tn),
        in_specs=[
            pl.BlockSpec((SUBLANES, d), lambda l, j: (0, 0)),
            pl.BlockSpec((1, d, tn), lambda l, j: (l, 0, j)),
            pl.BlockSpec((1, 1, tn), lambda l, j: (l, 0, j)),
        ],
        out_specs=pl.BlockSpec((1, SUBLANES, tn), lambda l, j: (l, 0, j)),
        out_shape=jax.ShapeDtypeStruct((depth, SUBLANES, n), F32),
        compiler_params=_cparams("parallel", "parallel"),
        name="adaln",
    )(cvec, w_ada, b_ada.reshape(depth, 1, n))


def _dup_halves(t):
    r = pltpu.roll(t, 64, axis=1)
    lo = lax.broadcasted_iota(jnp.int32, t.shape, 1) < 64
    return jnp.concatenate([jnp.where(lo, t, r), jnp.where(lo, r, t)], axis=1)


def _rope(t, cos, sin_a, sin_b):
    out = []
    for j in range(t.shape[1] // LANES):
        blk = t[:, j * LANES:(j + 1) * LANES]
        out.append(blk * cos + pltpu.roll(blk, LANES - 16, axis=1) * sin_a
                   + pltpu.roll(blk, 16, axis=1) * sin_b)
    return jnp.concatenate(out, axis=1) if len(out) > 1 else out[0]


def _proj_kernel(*refs, rope, kv_only):
    if rope:
        x_ref, sh_ref, sc_ref, g_ref, w_ref, cos_ref, sa_ref, sb_ref = refs[:8]
        outs = refs[8:]
    else:
        x_ref, sh_ref, sc_ref, g_ref, w_ref = refs[:5]
        outs = refs[5:]
    hb = _norm_mod(x_ref[0], g_ref[...], sh_ref[0], sc_ref[0]).astype(BF16)

    if kv_only:
        k_ref, v_ref = outs
        kv = _dot(hb, w_ref[:, Q_W:Q_W + 2 * KV_W])
        k_ref[0] = _dup_halves(kv[:, :KV_W]).astype(BF16)
        v_ref[0] = _dup_halves(kv[:, KV_W:]).astype(BF16)
        return

    q_ref, k_ref, v_ref, u0_ref, pb_ref, cx_ref, fn_ref = outs
    qkv = _dot(hb, w_ref[:, 0:Q_W + 2 * KV_W])
    q = qkv[:, :Q_W]
    k = qkv[:, Q_W:Q_W + KV_W]
    v = qkv[:, Q_W + KV_W:]
    if rope:
        cos, sa, sb = cos_ref[...], sa_ref[...], sb_ref[...]
        q = _rope(q, cos, sa, sb)
        k = _rope(k, cos, sa, sb)
    q_ref[0] = (q * (HEAD_DIM ** -0.5)).astype(BF16)
    k_ref[0] = _dup_halves(k).astype(BF16)
    v_ref[0] = _dup_halves(v).astype(BF16)

    c0 = Q_W + 2 * KV_W
    conf = _dot(hb, w_ref[:, c0:c0 + 2 * CONF_W])
    u0_ref[0] = (conf[:, :CONF_W] * _sigmoid(conf[:, CONF_W:])).astype(BF16)
    c0 += 2 * CONF_W
    pb_ref[0] = _dot(hb, w_ref[:, c0:c0 + SC_W]).astype(BF16)
    c0 += SC_W
    pcx = _dot(hb, w_ref[:, c0:c0 + 2 * SC_W])
    cx_ref[0] = (pcx[:, :SC_W] * pcx[:, SC_W:]).astype(BF16)
    c0 += 2 * SC_W
    fn_ref[0] = _dot(hb, w_ref[:, c0:c0 + FN_W])


def _proj_call(x, shift, scale, g, w_main, rope_tabs, *, tm, kv_only=False):
    b, s, d = x.shape
    nt = s // tm
    rope = rope_tabs is not None
    mod_spec = pl.BlockSpec((1, 1, d), (lambda bi, i: (bi, 0, 0)) if shift.shape[0] > 1
                            else (lambda bi, i: (0, 0, 0)))
    in_specs = [
        pl.BlockSpec((1, tm, d), lambda bi, i: (bi, i, 0)),
        mod_spec, mod_spec,
        _const_spec((1, d)),
        _const_spec(w_main.shape),
    ]
    args = [x, shift, scale, g, w_main]
    if rope:
        in_specs += [pl.BlockSpec((tm, LANES), lambda bi, i: (i, 0))] * 3
        args += list(rope_tabs)

    def out(width, dtype):
        return (pl.BlockSpec((1, tm, width), lambda bi, i: (bi, i, 0)),
                jax.ShapeDtypeStruct((b, s, width), dtype))

    if kv_only:
        outs = [out(2 * KV_W, BF16), out(2 * KV_W, BF16)]
    else:
        outs = [out(Q_W, BF16), out(2 * KV_W, BF16), out(2 * KV_W, BF16), out(CONF_W, BF16),
                out(SC_W, BF16), out(SC_W, BF16), out(FN_W, F32)]
    return pl.pallas_call(
        functools.partial(_proj_kernel, rope=rope, kv_only=kv_only),
        grid=(b, nt),
        in_specs=in_specs,
        out_specs=[o[0] for o in outs],
        out_shape=[o[1] for o in outs],
        compiler_params=_cparams("parallel", "parallel"),
        name="proj_kv" if kv_only else ("proj_rope" if rope else "proj"),
    )(*args)


def _attend(q, kd, vd, mask, sink_ref):
    rows = q.shape[0]
    group = N_HEADS // N_KV_HEADS
    lo = lax.broadcasted_iota(jnp.int32, (rows, LANES), 1) < 64
    qf = q.astype(F32)
    out_blocks = []
    for hk in range(N_KV_HEADS):
        kh = kd[:, hk * LANES:(hk + 1) * LANES]
        vh = vd[:, hk * LANES:(hk + 1) * LANES]
        qs, sks = [], []
        for p in range(group // 2):
            blk = qf[:, (2 * hk + p) * LANES:(2 * hk + p + 1) * LANES]
            qs.append(jnp.where(lo, blk, 0.0))
            qs.append(jnp.where(lo, 0.0, blk))
        for g in range(group):
            hd = group * hk + g
            sks.append(jnp.broadcast_to(sink_ref[hd:hd + 1, 0:1], (rows, 1)))
        qst = jnp.concatenate(qs, axis=0).astype(BF16)
        sk = jnp.concatenate(sks, axis=0)
        s = lax.dot_general(qst, kh, (((1,), (1,)), ((), ())), preferred_element_type=F32)
        if mask is not None:
            s = jnp.where(mask, s, NEG_INF)
        m = jnp.maximum(jnp.max(s, axis=-1, keepdims=True), sk)
        p_ = jnp.exp(s - m)
        denom = jnp.sum(p_, axis=-1, keepdims=True) + jnp.exp(sk - m)
        o = _dot(p_.astype(BF16), vh) / denom
        out_blocks.append(jnp.where(lo, o[0:rows], o[rows:2 * rows]))
        out_blocks.append(jnp.where(lo, o[2 * rows:3 * rows], o[3 * rows:4 * rows]))
    return jnp.concatenate(out_blocks, axis=1)


def _attn_kernel(q_ref, kp_ref, kc_ref, kn_ref, vp_ref, vc_ref, vn_ref, kx_ref, vx_ref, sink_ref,
                 o_ref, *, nb):
    i = pl.program_id(1)
    lc = kx_ref.shape[1]
    group = N_HEADS // N_KV_HEADS
    shape = (group * BLOCK, 3 * BLOCK + lc)
    row = lax.broadcasted_iota(jnp.int32, shape, 0) & (BLOCK - 1)
    col = lax.broadcasted_iota(jnp.int32, shape, 1)
    is_prev = col < BLOCK
    is_next = jnp.logical_and(col >= 2 * BLOCK, col < 3 * BLOCK)
    neither = jnp.logical_not(jnp.logical_or(is_prev, is_next))
    kall = jnp.concatenate([kp_ref[0], kc_ref[0], kn_ref[0]], axis=0)
    vall = jnp.concatenate([vp_ref[0], vc_ref[0], vn_ref[0]], axis=0)
    for sb in range(ATTN_QB):
        blk = i * ATTN_QB + sb
        off_p = jnp.where(blk > 0, 0, BLOCK)
        off_n = jnp.where(blk < nb - 1, 0, BLOCK)
        ok_prev = col >= row + off_p
        ok_next = col - 2 * BLOCK <= row - off_n
        mask = jnp.logical_or(
            jnp.logical_or(jnp.logical_and(is_prev, ok_prev), jnp.logical_and(is_next, ok_next)),
            neither)
        band = slice(sb * BLOCK, (sb + 3) * BLOCK)
        kd = jnp.concatenate([kall[band], kx_ref[0]], axis=0)
        vd = jnp.concatenate([vall[band], vx_ref[0]], axis=0)
        rows = slice(sb * BLOCK, (sb + 1) * BLOCK)
        o_ref[0, rows, :] = _attend(q_ref[0, rows, :], kd, vd, mask, sink_ref).astype(BF16)


def _attn_call(q, k, v, kx, vx, sink_b):
    b, s, _ = q.shape
    nb = s // BLOCK
    lc = kx.shape[1]
    kvw = 2 * KV_W
    qb = ATTN_QB
    assert nb % qb == 0

    def edge_spec(off):
        def idx(bi, i):
            return (bi, jnp.clip(i * qb + off, 0, nb - 1), 0)
        return pl.BlockSpec((1, BLOCK, kvw), idx)

    main_spec = pl.BlockSpec((1, qb * BLOCK, kvw), lambda bi, i: (bi, i, 0))
    ctx_spec = pl.BlockSpec((1, lc, kvw), lambda bi, i: (bi, 0, 0))
    return pl.pallas_call(
        functools.partial(_attn_kernel, nb=nb),
        grid=(b, nb // qb),
        in_specs=[pl.BlockSpec((1, qb * BLOCK, Q_W), lambda bi, i: (bi, i, 0)),
                  edge_spec(-1), main_spec, edge_spec(qb), edge_spec(-1), main_spec, edge_spec(qb),
                  ctx_spec, ctx_spec, _const_spec(sink_b.shape)],
        out_specs=pl.BlockSpec((1, qb * BLOCK, Q_W), lambda bi, i: (bi, i, 0)),
        out_shape=jax.ShapeDtypeStruct((b, s, Q_W), BF16),
        compiler_params=_cparams("parallel", "parallel"),
        name="attn_window",
    )(q, k, k, k, v, v, v, kx, vx, sink_b)


def _attn_ctx_kernel(q_ref, k_ref, v_ref, sink_ref, o_ref):
    o_ref[0] = _attend(q_ref[0], k_ref[0], v_ref[0], None, sink_ref).astype(BF16)


def _attn_ctx_call(q, k, v, sink_b):
    b, lc, _ = q.shape
    kvw = 2 * KV_W
    return pl.pallas_call(
        _attn_ctx_kernel,
        grid=(b,),
        in_specs=[pl.BlockSpec((1, lc, Q_W), lambda bi: (bi, 0, 0)),
                  pl.BlockSpec((1, lc, kvw), lambda bi: (bi, 0, 0)),
                  pl.BlockSpec((1, lc, kvw), lambda bi: (bi, 0, 0)),
                  _const_spec(sink_b.shape)],
        out_specs=pl.BlockSpec((1, lc, Q_W), lambda bi: (bi, 0, 0)),
        out_shape=jax.ShapeDtypeStruct((b, lc, Q_W), BF16),
        compiler_params=_cparams("parallel"),
        name="attn_ctx",
    )(q, k, v, sink_b)


def _dft_consts(n_pos):
    r = int(round(np.sqrt(n_pos)))
    assert n_pos == r * r and r % SUBLANES == 0
    idx = np.arange(r)
    ang = 2.0 * np.pi * np.outer(idx, idx) / r
    c, s = np.cos(ang), np.sin(ang)
    eye = np.eye(SUBLANES)
    m1 = np.concatenate([np.kron(c, eye), np.kron(-s, eye)], axis=0)
    def perm_kron(f):
        return np.einsum("ks,lm->klms", f, eye).reshape(r * SUBLANES, SUBLANES * r)
    m2 = np.block([[perm_kron(c), perm_kron(s)], [perm_kron(-s), perm_kron(c)]])
    sfull = np.arange(r).reshape(r // SUBLANES, 1, SUBLANES)
    tw = 2.0 * np.pi * idx.reshape(1, r, 1) * sfull / n_pos
    tw = tw.reshape(r // SUBLANES, r * SUBLANES, 1)
    twc = np.broadcast_to(np.cos(tw), tw.shape[:2] + (LANES,))
    tws = np.broadcast_to(np.sin(tw), tw.shape[:2] + (LANES,))
    return (m1.astype(np.float32), m2.astype(np.float32),
            np.ascontiguousarray(twc, np.float32), np.ascontiguousarray(tws, np.float32))


def _chan_consts():
    idx = np.arange(FN_GROUP_W)
    ang = 2.0 * np.pi * np.outer(idx, idx) / FN_GROUP_W
    groups = FN_W // FN_GROUP_W
    cc = np.kron(np.eye(groups), np.cos(ang))
    sc = np.kron(np.eye(groups), np.sin(ang))
    return np.concatenate([cc, sc], axis=0).astype(np.float32)


def _fft1_kernel(x_ref, m1_ref, twc_ref, tws_ref, o_ref):
    r = x_ref.shape[1]
    x = x_ref[0].reshape(r * SUBLANES, FN_W).astype(BF16)
    t = _dot(m1_ref[...], x)
    half = r * SUBLANES
    tr, ti = t[:half], t[half:]
    cw = jnp.concatenate([twc_ref[0]] * (FN_W // LANES), axis=1)
    sw = jnp.concatenate([tws_ref[0]] * (FN_W // LANES), axis=1)
    o_ref[0, 0] = (tr * cw + ti * sw).reshape(r, SUBLANES, FN_W)
    o_ref[0, 1] = (ti * cw - tr * sw).reshape(r, SUBLANES, FN_W)


def _fft2_kernel(t_ref, m2_ref, ch_ref, o_ref, *, scale):
    r = t_ref.shape[3]
    half = r * SUBLANES
    tr = t_ref[0, 0].reshape(half, FN_W)
    ti = t_ref[0, 1].reshape(half, FN_W)
    t = jnp.concatenate([tr, ti], axis=0).astype(BF16)
    z = _dot(m2_ref[...], t)
    zc = jnp.concatenate([z[:half], z[half:]], axis=1).astype(BF16)
    y = _dot(zc, ch_ref[...]) * scale
    o_ref[0] = y.reshape(r, SUBLANES, FN_W)


def _fft_call(fn, consts):
    m1, m2, twc, tws, ch = consts
    b, n, w = fn.shape
    r = m1.shape[1] // SUBLANES
    nblk = r // SUBLANES
    x4 = fn.reshape(b, r, r, w)
    t = pl.pallas_call(
        _fft1_kernel,
        grid=(b, nblk),
        in_specs=[pl.BlockSpec((1, r, SUBLANES, w), lambda bi, j: (bi, 0, j, 0)),
                  _const_spec(m1.shape),
                  pl.BlockSpec((1, r * SUBLANES, LANES), lambda bi, j: (j, 0, 0)),
                  pl.BlockSpec((1, r * SUBLANES, LANES), lambda bi, j: (j, 0, 0))],
        out_specs=pl.BlockSpec((1, 2, r, SUBLANES, w), lambda bi, j: (bi, 0, 0, j, 0)),
        out_shape=jax.ShapeDtypeStruct((b, 2, r, r, w), F32),
        compiler_params=_cparams("parallel", "parallel"),
        name="fft_stage1",
    )(x4, m1, twc, tws)
    y = pl.pallas_call(
        functools.partial(_fft2_kernel, scale=float(1.0 / np.sqrt(n * FN_GROUP_W))),
        grid=(b, nblk),
        in_specs=[pl.BlockSpec((1, 2, SUBLANES, r, w), lambda bi, j: (bi, 0, j, 0, 0)),
                  _const_spec(m2.shape),
                  _const_spec(ch.shape)],
        out_specs=pl.BlockSpec((1, r, SUBLANES, w), lambda bi, j: (bi, 0, j, 0)),
        out_shape=jax.ShapeDtypeStruct((b, r, r, w), F32),
        compiler_params=_cparams("parallel", "parallel"),
        name="fft_stage2",
    )(t, m2, ch)
    return y.reshape(b, n, w)


def _fft_small_kernel(x_ref, m_ref, ch_ref, o_ref, *, scale):
    n = x_ref.shape[1]
    z = _dot(m_ref[...], x_ref[0].astype(BF16))
    zc = jnp.concatenate([z[:n], z[n:]], axis=1).astype(BF16)
    o_ref[0] = _dot(zc, ch_ref[...]) * scale


def _fft_small_call(fn, m, ch):
    b, n, w = fn.shape
    return pl.pallas_call(
        functools.partial(_fft_small_kernel, scale=float(1.0 / np.sqrt(n * FN_GROUP_W))),
        grid=(b,),
        in_specs=[pl.BlockSpec((1, n, w), lambda bi: (bi, 0, 0)),
                  _const_spec(m.shape), _const_spec(ch.shape)],
        out_specs=pl.BlockSpec((1, n, w), lambda bi: (bi, 0, 0)),
        out_shape=jax.ShapeDtypeStruct((b, n, w), F32),
        compiler_params=_cparams("parallel"),
        name="fft_ctx",
    )(fn, m, ch)


def _small_dft_const(n):
    idx = np.arange(n)
    ang = 2.0 * np.pi * np.outer(idx, idx) / n
    return np.concatenate([np.cos(ang), -np.sin(ang)], axis=0).astype(np.float32)


HALO = 16


def _merge_kernel(x_ref, sh_ref, sc_ref, gt_ref, g_ref, ya_ref,
                  u0p_ref, u0_ref, u0n_ref, cxp_ref, cx_ref, cxn_ref, pb_ref, yf_ref,
                  wg_ref, wba_ref, wbc_ref, wbs_ref, wbf_ref, wo_ref,
                  cw_ref, cb_ref, lg_ref, lb_ref, scw_ref,
                  o_ref, *, tm, nt):
    i = pl.program_id(1)
    first = i == 0
    last = i == nt - 1
    x = x_ref[0]
    hb = _norm_mod(x, g_ref[...], sh_ref[0], sc_ref[0]).astype(BF16)

    def extended(p_ref, c_ref, n_ref):
        return jnp.concatenate([jnp.where(first, 0.0, p_ref[0].astype(F32)),
                                c_ref[0].astype(F32),
                                jnp.where(last, 0.0, n_ref[0].astype(F32))], axis=0)

    def dwconv(ext, w_ref, taps, init):
        n = ext.shape[0]
        pad = (taps - 1) // 2
        rolled = {}
        acc = init
        for j in range(taps):
            q, rho = divmod(HALO - pad + j, SUBLANES)
            if rho not in rolled:
                rolled[rho] = ext if rho == 0 else pltpu.roll(ext, n - rho, axis=0)
            acc = acc + w_ref[j:j + 1, :] * rolled[rho][q * SUBLANES:q * SUBLANES + tm]
        return acc

    def conf_branch():
        acc = dwconv(extended(u0p_ref, u0_ref, u0n_ref), cw_ref, CONF_K,
                     jnp.zeros((tm, CONF_W), F32) + cb_ref[...])
        mu = jnp.mean(acc, axis=-1, keepdims=True)
        xc = acc - mu
        var = jnp.mean(xc * xc, axis=-1, keepdims=True)
        yn = xc * lax.rsqrt(var + EPS) * lg_ref[...] + lb_ref[...]
        return (yn * _sigmoid(yn)).astype(BF16)

    def sconv_branch():
        conv = dwconv(extended(cxp_ref, cx_ref, cxn_ref), scw_ref, SC_K,
                      jnp.zeros((tm, SC_W), F32))
        return (pb_ref[0].astype(F32) * conv).astype(BF16)

    ys = {0: lambda: ya_ref[0], 3: lambda: yf_ref[0].astype(BF16), 2: sconv_branch, 1: conf_branch}
    wbs = (wba_ref, wbc_ref, wbs_ref, wbf_ref)
    order = (0, 3, 2, 1)
    chunks = range(0, D_MODEL, MERGE_CHUNK)
    items = [(idx, n0) for idx in order for n0 in chunks]
    y_val = {}

    def issue(item):
        idx, n0 = item
        if idx not in y_val:
            y_val[idx] = ys[idx]()
        c0 = idx * D_MODEL + n0
        return (_dot(hb, wg_ref[:, c0:c0 + MERGE_CHUNK]),
                _dot(y_val[idx], wbs[idx][:, n0:n0 + MERGE_CHUNK]))

    m = {}
    nxt = issue(items[0])
    y_val[2] = ys[2]()
    y_val[1] = ys[1]()
    for k, (idx, n0) in enumerate(items):
        pre, br = nxt
        if k + 1 < len(items):
            nxt = issue(items[k + 1])
        term = _sigmoid(pre) * br
        m[n0] = term if idx == order[0] else m[n0] + term
    mb = jnp.concatenate([m[n0].astype(BF16) for n0 in chunks], axis=1)
    o_ref[0] = x + gt_ref[0] * _dot(mb, wo_ref[...])


def _merge_call(x, mods, g, ya, u0, cx, pb, yf, weights, *, tm):
    b, s, d = x.shape
    nt = s // tm
    shift, scale, gate = mods
    wg, wba, wbc, wbs, wbf, wo, cw, cb, lg, lb, scw = weights
    per_batch = shift.shape[0] > 1
    mod_spec = pl.BlockSpec((1, 1, d), (lambda bi, i: (bi, 0, 0)) if per_batch
                            else (lambda bi, i: (0, 0, 0)))
    hpt = tm // HALO
    nh = s // HALO

    def tile(width):
        return pl.BlockSpec((1, tm, width), lambda bi, i: (bi, i, 0))

    def halo(width, off):
        if off < 0:
            return pl.BlockSpec((1, HALO, width),
                                lambda bi, i: (bi, jnp.maximum(i * hpt - 1, 0), 0))
        return pl.BlockSpec((1, HALO, width),
                            lambda bi, i: (bi, jnp.minimum((i + 1) * hpt, nh - 1), 0))

    in_specs = [tile(d), mod_spec, mod_spec, mod_spec, _const_spec((1, d)), tile(Q_W),
                halo(CONF_W, -1), tile(CONF_W), halo(CONF_W, 1),
                halo(SC_W, -1), tile(SC_W), halo(SC_W, 1), tile(SC_W), tile(FN_W)]
    in_specs += [_const_spec(w.shape) for w in weights]
    return pl.pallas_call(
        functools.partial(_merge_kernel, tm=tm, nt=nt),
        grid=(b, nt),
        in_specs=in_specs,
        out_specs=tile(d),
        out_shape=jax.ShapeDtypeStruct((b, s, d), F32),
        compiler_params=_cparams("parallel", "parallel"),
        name="merge",
    )(x, shift, scale, gate, g, ya, u0, u0, u0, cx, cx, cx, pb, yf, *weights)


def _ffn_kernel(*refs, tm, nt, final):
    if final:
        (xp_ref, x_ref, xn_ref, sh_ref, sc_ref, gt_ref, g_ref, wu_ref, wd_ref, dw_ref, db_ref,
         gf_ref, o_ref, gact_ref) = refs
    else:
        (xp_ref, x_ref, xn_ref, sh_ref, sc_ref, gt_ref, g_ref, wu_ref, wd_ref, dw_ref, db_ref,
         o_ref, gact_ref) = refs
    i = pl.program_id(1)
    x = x_ref[0]
    ne = tm + 2 * SUBLANES
    xe = jnp.concatenate([xp_ref[0], x, xn_ref[0]], axis=0)
    he = _norm_mod(xe, g_ref[...], sh_ref[0], sc_ref[0])
    rid = lax.broadcasted_iota(jnp.int32, (ne, 1), 0)
    lo = jnp.where(i > 0, 0, SUBLANES)
    hi = jnp.where(i < nt - 1, ne, tm + SUBLANES)
    he = jnp.where(jnp.logical_and(rid >= lo, rid < hi), he, 0.0).astype(BF16)

    mid = slice(SUBLANES, SUBLANES + tm)
    def up(c):
        return (_dot(he, wu_ref[:, c:c + FFN_CHUNK]),
                _dot(he, wu_ref[:, D_FF + c:D_FF + c + FFN_CHUNK])[mid])

    nxt = up(0)
    for c in range(0, D_FF, FFN_CHUNK):
        a, bb = nxt
        if c + FFN_CHUNK < D_FF:
            nxt = up(c + FFN_CHUNK)
        conv = (db_ref[:, c:c + FFN_CHUNK]
                + dw_ref[0:1, c:c + FFN_CHUNK] * pltpu.roll(a, 1, axis=0)[mid]
                + dw_ref[1:2, c:c + FFN_CHUNK] * a[mid]
                + dw_ref[2:3, c:c + FFN_CHUNK] * pltpu.roll(a, ne - 1, axis=0)[mid])
        gact_ref[:, c:c + FFN_CHUNK] = (conv * _sigmoid(conv) * bb).astype(BF16)
    out = x + gt_ref[0] * _dot(gact_ref[...], wd_ref[...])
    if final:
        ms = jnp.mean(out * out, axis=-1, keepdims=True)
        out = out * lax.rsqrt(ms + EPS) * gf_ref[...]
    o_ref[0] = out


def _ffn_call(x, mods, g, wu, wd, dw, db, g_final, *, tm):
    b, s, d = x.shape
    nt = s // tm
    shift, scale, gate = mods
    per_batch = shift.shape[0] > 1
    mod_spec = pl.BlockSpec((1, 1, d), (lambda bi, i: (bi, 0, 0)) if per_batch
                            else (lambda bi, i: (0, 0, 0)))
    hpt = tm // SUBLANES
    nh = s // SUBLANES
    final = g_final is not None
    in_specs = [
        pl.BlockSpec((1, SUBLANES, d), lambda bi, i: (bi, jnp.maximum(i * hpt - 1, 0), 0)),
        pl.BlockSpec((1, tm, d), lambda bi, i: (bi, i, 0)),
        pl.BlockSpec((1, SUBLANES, d), lambda bi, i: (bi, jnp.minimum((i + 1) * hpt, nh - 1), 0)),
        mod_spec, mod_spec, mod_spec, _const_spec((1, d)),
        _const_spec(wu.shape), _const_spec(wd.shape), _const_spec(dw.shape), _const_spec(db.shape),
    ]
    args = [x, x, x, shift, scale, gate, g, wu, wd, dw, db]
    if final:
        in_specs.append(_const_spec((1, d)))
        args.append(g_final)
    return pl.pallas_call(
        functools.partial(_ffn_kernel, tm=tm, nt=nt, final=final),
        grid=(b, nt),
        in_specs=in_specs,
        out_specs=pl.BlockSpec((1, tm, d), lambda bi, i: (bi, i, 0)),
        out_shape=jax.ShapeDtypeStruct((b, s, d), F32),
        scratch_shapes=[pltpu.VMEM((tm, D_FF), BF16)],
        compiler_params=_cparams("parallel", "parallel"),
        name="ffn_final" if final else "ffn",
    )(*args)


def _rope_tables(s):
    n = np.arange(s)
    pos = np.stack([n // GRID_W, n % GRID_W], axis=0).astype(np.float64)
    inv = ROPE_BASE ** (-np.arange(ROPE_FREQS, dtype=np.float64) / ROPE_FREQS)
    lane = np.arange(LANES)
    axis = (lane % HEAD_DIM) // (2 * ROPE_FREQS)
    ang = pos[axis].T * inv[lane % ROPE_FREQS][None, :]
    first = (lane % (2 * ROPE_FREQS)) < ROPE_FREQS
    sin = np.sin(ang)
    return (np.cos(ang).astype(np.float32),
            np.where(first[None, :], -sin, 0.0).astype(np.float32),
            np.where(first[None, :], 0.0, sin).astype(np.float32))


def _pad_rows(w, rows):
    return jnp.pad(w, ((0, rows - w.shape[0]), (0, 0)))


def kernel(x, c, ctx, c_ctx, w_ada, b_ada, g_mix, w_in, attn_sink, conf_dw_w, conf_dw_b, conf_ln_g,
           conf_ln_b, sc_w, w_br_attn, w_br_conf, w_br_sconv, w_br_fourier, w_out, g_ffn, w_up,
           ffn_dw_w, ffn_dw_b, w_down, g_final):
    b, s, d = x.shape
    lc = ctx.shape[1]
    tm = 512

    cvec = jnp.concatenate([c, c_ctx[None, :], jnp.zeros((SUBLANES - b - 1, d), F32)], axis=0)
    mods = _ada_call(cvec, w_ada, b_ada)

    rope_tabs = tuple(jnp.asarray(t) for t in _rope_tables(s))
    m1, m2, twc, tws = _dft_consts(s)
    ch = jnp.asarray(_chan_consts()).astype(BF16)
    fft_consts = (jnp.asarray(m1).astype(BF16), jnp.asarray(m2).astype(BF16),
                  jnp.asarray(twc), jnp.asarray(tws), ch)
    m_ctx = jnp.asarray(_small_dft_const(lc)).astype(BF16)

    for l in range(DEPTH):
        last = l == DEPTH - 1
        ml = mods[l].reshape(SUBLANES, 6, d)
        lat = [ml[:b, k][:, None, :] for k in range(6)]
        cxm = [ml[b:b + 1, k][:, None, :] for k in range(6)]
        w_in_l = w_in[l].astype(BF16)
        w_main, w_gate = w_in_l[:, :MAIN_W], w_in_l[:, MAIN_W:]
        g1 = g_mix[l][None, :]
        g2 = g_ffn[l][None, :]
        sink_b = jnp.broadcast_to(attn_sink[l][:, None], (N_HEADS, LANES))
        merge_w = (w_gate, w_br_attn[l].astype(BF16), w_br_conf[l].astype(BF16),
                   w_br_sconv[l].astype(BF16), w_br_fourier[l].astype(BF16), w_out[l].astype(BF16),
                   _pad_rows(conf_dw_w[l], 32), conf_dw_b[l][None, :], conf_ln_g[l][None, :],
                   conf_ln_b[l][None, :], _pad_rows(sc_w[l], SUBLANES))
        wu, wd = w_up[l].astype(BF16), w_down[l].astype(BF16)
        dw, db = _pad_rows(ffn_dw_w[l], SUBLANES), ffn_dw_b[l][None, :]

        q, k, v, u0, pb, cx, fn = _proj_call(x, lat[0], lat[1], g1, w_main, rope_tabs, tm=tm)
        if last:
            kc, vc = _proj_call(ctx, cxm[0], cxm[1], g1, w_main, None, tm=lc, kv_only=True)
        else:
            qc, kc, vc, u0c, pbc, cxc, fnc = _proj_call(ctx, cxm[0], cxm[1], g1, w_main, None, tm=lc)

        ya = _attn_call(q, k, v, kc, vc, sink_b)
        yf = _fft_call(fn, fft_consts)
        x = _merge_call(x, (lat[0], lat[1], lat[2]), g1, ya, u0, cx, pb, yf, merge_w, tm=tm)
        if not last:
            yac = _attn_ctx_call(qc, kc, vc, sink_b)
            yfc = _fft_small_call(fnc, m_ctx, ch)
            ctx = _merge_call(ctx, (cxm[0], cxm[1], cxm[2]), g1, yac, u0c, cxc, pbc, yfc, merge_w, tm=lc)

        x = _ffn_call(x, (lat[3], lat[4], lat[5]), g2, wu, wd, dw, db,
                      g_final[None, :] if last else None, tm=tm)
        if not last:
            ctx = _ffn_call(ctx, (cxm[3], cxm[4], cxm[5]), g2, wu, wd, dw, db, None, tm=lc)
    return x
```

```python
import functools

import numpy as np
import jax
import jax.numpy as jnp
from jax import lax
from jax.experimental import pallas as pl
from jax.experimental.pallas import tpu as pltpu

F32 = jnp.float32
BF16 = jnp.bfloat16

D_MODEL = 1024
DEPTH = 4
GRID_W = 64
N_HEADS = 8
N_KV_HEADS = 2
HEAD_DIM = 64
WINDOW = 128
BLOCK = 128
ROPE_BASE = 10000.0
ROPE_FREQS = HEAD_DIM // 4
CONF_W = 256
CONF_K = 31
SC_W = 256
SC_K = 3
FN_W = 256
FN_GROUP_W = 64
N_BRANCH = 4
Q_W = N_HEADS * HEAD_DIM
KV_W = N_KV_HEADS * HEAD_DIM
MAIN_W = Q_W + 2 * KV_W + 2 * CONF_W + 3 * SC_W + FN_W
D_FF = 2816
EPS = 1e-6
NEG_INF = -1e30

LANES = 128
SUBLANES = 8
BF16_ROWS = 16
VMEM_LIMIT = 56 * 1024 * 1024
FFN_CHUNK = 256
MERGE_CHUNK = 256
ATTN_QB = 4


def _cparams(*sem):
    return pltpu.CompilerParams(dimension_semantics=sem, vmem_limit_bytes=VMEM_LIMIT)


def _const_spec(shape):
    nd = len(shape)
    return pl.BlockSpec(shape, lambda *_: (0,) * nd, pipeline_mode=pl.Buffered(1))


def _sigmoid(x):
    return 1.0 / (1.0 + jnp.exp(-x))


def _norm_mod(x, g, shift, scale):
    ms = jnp.mean(x * x, axis=-1, keepdims=True)
    y = x * lax.rsqrt(ms + EPS) * g
    return y * (1.0 + scale) + shift


def _dot(a, b):
    return jnp.dot(a, b, preferred_element_type=F32)


def _ada_kernel(c_ref, w_ref, b_ref, o_ref):
    c = c_ref[...]
    a = (c * _sigmoid(c)).astype(BF16)
    o_ref[...] = _dot(a, w_ref[...].astype(BF16)) + b_ref[...]


def _ada_call(cvec, w_ada, b_ada):
    depth, d, n = w_ada.shape
    nk = n // d
    return pl.pallas_call(
        _ada_kernel,
        grid=(depth, nk),
        in_specs=[
            pl.BlockSpec((SUBLANES, d), lambda l, k: (0, 0)),
            pl.BlockSpec((None, d, d), lambda l, k: (l, 0, k)),
            pl.BlockSpec((None, None, 1, d), lambda l, k: (l, k, 0, 0)),
        ],
        out_specs=pl.BlockSpec((None, None, SUBLANES, d), lambda l, k: (l, k, 0, 0)),
        out_shape=jax.ShapeDtypeStruct((depth, nk, SUBLANES, d), F32),
        compiler_params=_cparams("parallel", "parallel"),
        name="adaln",
    )(cvec, w_ada, b_ada.reshape(depth, nk, 1, d))


def _layer_spec(arr, l):
    return pl.BlockSpec((None,) + arr.shape[1:], lambda *_: (l, 0, 0), pipeline_mode=pl.Buffered(1))


def _mod_spec(mods, l, k):
    return pl.BlockSpec((None, None) + mods.shape[2:], lambda *_: (l, k, 0, 0))


def _mod_row(ref, row):
    r = pl.program_id(0) if row is None else row
    return ref[pl.ds(r, 1), :]


def _dup_halves(t):
    r = pltpu.roll(t, 64, axis=1)
    lo = lax.broadcasted_iota(jnp.int32, t.shape, 1) < 64
    return jnp.concatenate([jnp.where(lo, t, r), jnp.where(lo, r, t)], axis=1)


def _rope(t, cos, sin_a, sin_b):
    out = []
    for j in range(t.shape[1] // LANES):
        blk = t[:, j * LANES:(j + 1) * LANES]
        out.append(blk * cos + pltpu.roll(blk, LANES - 16, axis=1) * sin_a
                   + pltpu.roll(blk, 16, axis=1) * sin_b)
    return jnp.concatenate(out, axis=1) if len(out) > 1 else out[0]


def _proj_kernel(*refs, rope, kv_only, mod_row):
    if rope:
        x_ref, sh_ref, sc_ref, g_ref, w_ref, cos_ref, sa_ref, sb_ref = refs[:8]
        outs = refs[8:]
    else:
        x_ref, sh_ref, sc_ref, g_ref, w_ref = refs[:5]
        outs = refs[5:]
    hb = _norm_mod(x_ref[0], g_ref[...], _mod_row(sh_ref, mod_row),
                   _mod_row(sc_ref, mod_row)).astype(BF16)

    if kv_only:
        k_ref, v_ref = outs
        kv = _dot(hb, w_ref[:, Q_W:Q_W + 2 * KV_W])
        k_ref[0] = _dup_halves(kv[:, :KV_W]).astype(BF16)
        v_ref[0] = _dup_halves(kv[:, KV_W:]).astype(BF16)
        return

    q_ref, k_ref, v_ref, u0_ref, pb_ref, cx_ref, fn_ref = outs
    qkv = _dot(hb, w_ref[:, 0:Q_W + 2 * KV_W])
    q = qkv[:, :Q_W]
    k = qkv[:, Q_W:Q_W + KV_W]
    v = qkv[:, Q_W + KV_W:]
    if rope:
        cos, sa, sb = cos_ref[...], sa_ref[...], sb_ref[...]
        q = _rope(q, cos, sa, sb)
        k = _rope(k, cos, sa, sb)
    q_ref[0] = (q * (HEAD_DIM ** -0.5)).astype(BF16)
    k_ref[0] = _dup_halves(k).astype(BF16)
    v_ref[0] = _dup_halves(v).astype(BF16)

    c0 = Q_W + 2 * KV_W
    conf = _dot(hb, w_ref[:, c0:c0 + 2 * CONF_W])
    u0_ref[0] = (conf[:, :CONF_W] * _sigmoid(conf[:, CONF_W:])).astype(BF16)
    c0 += 2 * CONF_W
    pb_ref[0] = _dot(hb, w_ref[:, c0:c0 + SC_W]).astype(BF16)
    c0 += SC_W
    pcx = _dot(hb, w_ref[:, c0:c0 + 2 * SC_W])
    cx_ref[0] = (pcx[:, :SC_W] * pcx[:, SC_W:]).astype(BF16)
    c0 += 2 * SC_W
    fn_ref[0] = _dot(hb, w_ref[:, c0:c0 + FN_W])


def _proj_call(x, mods, g, w_in, rope_tabs, l, *, tm, mod_row=None, kv_only=False):
    b, s, d = x.shape
    nt = s // tm
    rope = rope_tabs is not None
    in_specs = [
        pl.BlockSpec((1, tm, d), lambda bi, i: (bi, i, 0)),
        _mod_spec(mods, l, 0), _mod_spec(mods, l, 1),
        _layer_spec(g, l),
        _layer_spec(w_in, l),
    ]
    args = [x, mods, mods, g, w_in]
    if rope:
        in_specs += [pl.BlockSpec((tm, LANES), lambda bi, i: (i, 0))] * 3
        args += list(rope_tabs)

    def out(width, dtype):
        return (pl.BlockSpec((1, tm, width), lambda bi, i: (bi, i, 0)),
                jax.ShapeDtypeStruct((b, s, width), dtype))

    if kv_only:
        outs = [out(2 * KV_W, BF16), out(2 * KV_W, BF16)]
    else:
        outs = [out(Q_W, BF16), out(2 * KV_W, BF16), out(2 * KV_W, BF16), out(CONF_W, BF16),
                out(SC_W, BF16), out(SC_W, BF16), out(FN_W, F32)]
    return pl.pallas_call(
        functools.partial(_proj_kernel, rope=rope, kv_only=kv_only, mod_row=mod_row),
        grid=(b, nt),
        in_specs=in_specs,
        out_specs=[o[0] for o in outs],
        out_shape=[o[1] for o in outs],
        compiler_params=_cparams("parallel", "parallel"),
        name="proj_kv" if kv_only else ("proj_rope" if rope else "proj"),
    )(*args)


def _attend(q, kd, vd, mask, sink_ref):
    scores = [_scores(q, kd, hk) for hk in range(N_KV_HEADS)]
    blocks = []
    for hk in range(N_KV_HEADS):
        blocks += _softmax_pv(scores[hk], mask, vd, hk, sink_ref)
    return jnp.concatenate(blocks, axis=1)


def _scores(q, kd, hk):
    rows = q.shape[0]
    lo = lax.broadcasted_iota(jnp.int32, (rows, LANES), 1) < 64
    qf = q.astype(F32)
    qs = []
    for p in range(N_HEADS // N_KV_HEADS // 2):
        blk = qf[:, (2 * hk + p) * LANES:(2 * hk + p + 1) * LANES]
        qs.append(jnp.where(lo, blk, 0.0))
        qs.append(jnp.where(lo, 0.0, blk))
    qst = jnp.concatenate(qs, axis=0).astype(BF16)
    kh = kd[:, hk * LANES:(hk + 1) * LANES]
    return lax.dot_general(qst, kh, (((1,), (1,)), ((), ())), preferred_element_type=F32)


def _softmax_pv(s, mask, vd, hk, sink_ref):
    group = N_HEADS // N_KV_HEADS
    rows = s.shape[0] // group
    lo = lax.broadcasted_iota(jnp.int32, (rows, LANES), 1) < 64
    sk = jnp.concatenate(
        [jnp.broadcast_to(sink_ref[group * hk + g:group * hk + g + 1, 0:1], (rows, 1))
         for g in range(group)], axis=0)
    if mask is not None:
        s = jnp.where(mask, s, NEG_INF)
    m = jnp.maximum(jnp.max(s, axis=-1, keepdims=True), sk)
    p_ = jnp.exp(s - m)
    denom = jnp.sum(p_, axis=-1, keepdims=True) + jnp.exp(sk - m)
    o = _dot(p_.astype(BF16), vd[:, hk * LANES:(hk + 1) * LANES]) / denom
    return [jnp.where(lo, o[0:rows], o[rows:2 * rows]),
            jnp.where(lo, o[2 * rows:3 * rows], o[3 * rows:4 * rows])]


def _attn_kernel(q_ref, kp_ref, kc_ref, kn_ref, vp_ref, vc_ref, vn_ref, kx_ref, vx_ref, sink_ref,
                 o_ref, *, nb):
    i = pl.program_id(1)
    lc = kx_ref.shape[1]
    group = N_HEADS // N_KV_HEADS
    shape = (group * BLOCK, 3 * BLOCK + lc)
    row = lax.broadcasted_iota(jnp.int32, shape, 0) & (BLOCK - 1)
    col = lax.broadcasted_iota(jnp.int32, shape, 1)
    is_prev = col < BLOCK
    is_next = jnp.logical_and(col >= 2 * BLOCK, col < 3 * BLOCK)
    neither = jnp.logical_not(jnp.logical_or(is_prev, is_next))
    kall = jnp.concatenate([kp_ref[0], kc_ref[0], kn_ref[0]], axis=0)
    vall = jnp.concatenate([vp_ref[0], vc_ref[0], vn_ref[0]], axis=0)
    scores = {}
    for sb in range(ATTN_QB):
        kd = jnp.concatenate([kall[sb * BLOCK:(sb + 3) * BLOCK], kx_ref[0]], axis=0)
        for hk in range(N_KV_HEADS):
            scores[sb, hk] = _scores(q_ref[0, sb * BLOCK:(sb + 1) * BLOCK, :], kd, hk)
    for sb in range(ATTN_QB):
        blk = i * ATTN_QB + sb
        off_p = jnp.where(blk > 0, 0, BLOCK)
        off_n = jnp.where(blk < nb - 1, 0, BLOCK)
        ok_prev = col >= row + off_p
        ok_next = col - 2 * BLOCK <= row - off_n
        mask = jnp.logical_or(
            jnp.logical_or(jnp.logical_and(is_prev, ok_prev), jnp.logical_and(is_next, ok_next)),
            neither)
        vd = jnp.concatenate([vall[sb * BLOCK:(sb + 3) * BLOCK], vx_ref[0]], axis=0)
        blocks = []
        for hk in range(N_KV_HEADS):
            blocks += _softmax_pv(scores[sb, hk], mask, vd, hk, sink_ref)
        o_ref[0, sb * BLOCK:(sb + 1) * BLOCK, :] = jnp.concatenate(blocks, axis=1).astype(BF16)


def _attn_call(q, k, v, kx, vx, sink_b, l):
    b, s, _ = q.shape
    nb = s // BLOCK
    lc = kx.shape[1]
    kvw = 2 * KV_W
    qb = ATTN_QB
    assert nb % qb == 0

    def edge_spec(off):
        def idx(bi, i):
            return (bi, jnp.clip(i * qb + off, 0, nb - 1), 0)
        return pl.BlockSpec((1, BLOCK, kvw), idx)

    main_spec = pl.BlockSpec((1, qb * BLOCK, kvw), lambda bi, i: (bi, i, 0))
    ctx_spec = pl.BlockSpec((1, lc, kvw), lambda bi, i: (bi, 0, 0))
    return pl.pallas_call(
        functools.partial(_attn_kernel, nb=nb),
        grid=(b, nb // qb),
        in_specs=[pl.BlockSpec((1, qb * BLOCK, Q_W), lambda bi, i: (bi, i, 0)),
                  edge_spec(-1), main_spec, edge_spec(qb), edge_spec(-1), main_spec, edge_spec(qb),
                  ctx_spec, ctx_spec, _layer_spec(sink_b, l)],
        out_specs=pl.BlockSpec((1, qb * BLOCK, Q_W), lambda bi, i: (bi, i, 0)),
        out_shape=jax.ShapeDtypeStruct((b, s, Q_W), BF16),
        compiler_params=_cparams("parallel", "parallel"),
        name="attn_window",
    )(q, k, k, k, v, v, v, kx, vx, sink_b)


def _attn_ctx_kernel(q_ref, k_ref, v_ref, sink_ref, o_ref):
    o_ref[0] = _attend(q_ref[0], k_ref[0], v_ref[0], None, sink_ref).astype(BF16)


def _attn_ctx_call(q, k, v, sink_b, l):
    b, lc, _ = q.shape
    kvw = 2 * KV_W
    return pl.pallas_call(
        _attn_ctx_kernel,
        grid=(b,),
        in_specs=[pl.BlockSpec((1, lc, Q_W), lambda bi: (bi, 0, 0)),
                  pl.BlockSpec((1, lc, kvw), lambda bi: (bi, 0, 0)),
                  pl.BlockSpec((1, lc, kvw), lambda bi: (bi, 0, 0)),
                  _layer_spec(sink_b, l)],
        out_specs=pl.BlockSpec((1, lc, Q_W), lambda bi: (bi, 0, 0)),
        out_shape=jax.ShapeDtypeStruct((b, lc, Q_W), BF16),
        compiler_params=_cparams("parallel"),
        name="attn_ctx",
    )(q, k, v, sink_b)


def _dft_consts(n_pos):
    r = int(round(np.sqrt(n_pos)))
    assert n_pos == r * r and r % SUBLANES == 0
    idx = np.arange(r)
    ang = 2.0 * np.pi * np.outer(idx, idx) / r
    c, s = np.cos(ang), np.sin(ang)
    m1 = np.concatenate([c, -s], axis=0)
    m2 = np.block([[c, s], [-s, c]])
    tw = 2.0 * np.pi * np.outer(idx, idx) / n_pos
    twc = np.broadcast_to(np.cos(tw)[:, :, None], (r, r, LANES))
    tws = np.broadcast_to(np.sin(tw)[:, :, None], (r, r, LANES))
    return (m1.astype(np.float32), m2.astype(np.float32),
            np.ascontiguousarray(twc, np.float32), np.ascontiguousarray(tws, np.float32))


def _chan_consts():
    idx = np.arange(FN_GROUP_W)
    ang = 2.0 * np.pi * np.outer(idx, idx) / FN_GROUP_W
    groups = FN_W // FN_GROUP_W
    cc = np.kron(np.eye(groups), np.cos(ang))
    sc = np.kron(np.eye(groups), np.sin(ang))
    return np.concatenate([cc, sc], axis=0).astype(np.float32)


def _fft1_kernel(x_ref, m1_ref, twc_ref, tws_ref, o_ref):
    r = x_ref.shape[1]
    xt = pltpu.einshape("asc->sac", x_ref[0])
    trs, tis = [], []
    for sl in range(SUBLANES):
        t = _dot(m1_ref[...], xt[sl].astype(BF16))
        tr, ti = t[:r], t[r:]
        cw = jnp.concatenate([twc_ref[sl]] * (FN_W // LANES), axis=1)
        sw = jnp.concatenate([tws_ref[sl]] * (FN_W // LANES), axis=1)
        trs.append(tr * cw + ti * sw)
        tis.append(ti * cw - tr * sw)
    o_ref[0, 0] = pltpu.einshape("skc->ksc", jnp.stack(trs, axis=0))
    o_ref[0, 1] = pltpu.einshape("skc->ksc", jnp.stack(tis, axis=0))


def _fft2_kernel(t_ref, m2_ref, ch_ref, o_ref, *, scale):
    r = t_ref.shape[3]
    ys = []
    for kl in range(SUBLANES):
        t = jnp.concatenate([t_ref[0, 0, kl], t_ref[0, 1, kl]], axis=0).astype(BF16)
        z = _dot(m2_ref[...], t)
        zc = jnp.concatenate([z[:r], z[r:]], axis=1).astype(BF16)
        ys.append(_dot(zc, ch_ref[...]) * scale)
    o_ref[0] = pltpu.einshape("kjc->jkc", jnp.stack(ys, axis=0))


def _fft_call(fn, consts):
    m1, m2, twc, tws, ch = consts
    b, n, w = fn.shape
    r = m1.shape[1]
    nblk = r // SUBLANES
    x4 = fn.reshape(b, r, r, w)
    t = pl.pallas_call(
        _fft1_kernel,
        grid=(b, nblk),
        in_specs=[pl.BlockSpec((1, r, SUBLANES, w), lambda bi, j: (bi, 0, j, 0)),
                  _const_spec(m1.shape),
                  pl.BlockSpec((SUBLANES, r, LANES), lambda bi, j: (j, 0, 0)),
                  pl.BlockSpec((SUBLANES, r, LANES), lambda bi, j: (j, 0, 0))],
        out_specs=pl.BlockSpec((1, 2, r, SUBLANES, w), lambda bi, j: (bi, 0, 0, j, 0)),
        out_shape=jax.ShapeDtypeStruct((b, 2, r, r, w), F32),
        compiler_params=_cparams("parallel", "parallel"),
        name="fft_stage1",
    )(x4, m1, twc, tws)
    y = pl.pallas_call(
        functools.partial(_fft2_kernel, scale=float(1.0 / np.sqrt(n * FN_GROUP_W))),
        grid=(b, nblk),
        in_specs=[pl.BlockSpec((1, 2, SUBLANES, r, w), lambda bi, j: (bi, 0, j, 0, 0)),
                  _const_spec(m2.shape),
                  _const_spec(ch.shape)],
        out_specs=pl.BlockSpec((1, r, SUBLANES, w), lambda bi, j: (bi, 0, j, 0)),
        out_shape=jax.ShapeDtypeStruct((b, r, r, w), F32),
        compiler_params=_cparams("parallel", "parallel"),
        name="fft_stage2",
    )(t, m2, ch)
    return y.reshape(b, n, w)


def _fft_small_kernel(x_ref, m_ref, ch_ref, o_ref, *, scale):
    n = x_ref.shape[1]
    z = _dot(m_ref[...], x_ref[0].astype(BF16))
    zc = jnp.concatenate([z[:n], z[n:]], axis=1).astype(BF16)
    o_ref[0] = _dot(zc, ch_ref[...]) * scale


def _fft_small_call(fn, m, ch):
    b, n, w = fn.shape
    return pl.pallas_call(
        functools.partial(_fft_small_kernel, scale=float(1.0 / np.sqrt(n * FN_GROUP_W))),
        grid=(b,),
        in_specs=[pl.BlockSpec((1, n, w), lambda bi: (bi, 0, 0)),
                  _const_spec(m.shape), _const_spec(ch.shape)],
        out_specs=pl.BlockSpec((1, n, w), lambda bi: (bi, 0, 0)),
        out_shape=jax.ShapeDtypeStruct((b, n, w), F32),
        compiler_params=_cparams("parallel"),
        name="fft_ctx",
    )(fn, m, ch)


def _small_dft_const(n):
    idx = np.arange(n)
    ang = 2.0 * np.pi * np.outer(idx, idx) / n
    return np.concatenate([np.cos(ang), -np.sin(ang)], axis=0).astype(np.float32)


HALO = 16


def _merge_kernel(x_ref, sh_ref, sc_ref, gt_ref, g_ref, ya_ref,
                  u0p_ref, u0_ref, u0n_ref, cxp_ref, cx_ref, cxn_ref, pb_ref, yf_ref,
                  wg_ref, wba_ref, wbc_ref, wbs_ref, wbf_ref, wo_ref,
                  cw_ref, cb_ref, lg_ref, lb_ref, scw_ref,
                  o_ref, *, tm, nt, mod_row):
    i = pl.program_id(1)
    first = i == 0
    last = i == nt - 1
    x = x_ref[0]
    hb = _norm_mod(x, g_ref[...], _mod_row(sh_ref, mod_row), _mod_row(sc_ref, mod_row)).astype(BF16)

    def extended(p_ref, c_ref, n_ref):
        return jnp.concatenate([jnp.where(first, 0.0, p_ref[0].astype(F32)),
                                c_ref[0].astype(F32),
                                jnp.where(last, 0.0, n_ref[0].astype(F32))], axis=0)

    def dwconv(ext, w_ref, taps, init):
        n = ext.shape[0]
        pad = (taps - 1) // 2
        rolled = {}
        acc = init
        for j in range(taps):
            q, rho = divmod(HALO - pad + j, SUBLANES)
            if rho not in rolled:
                rolled[rho] = ext if rho == 0 else pltpu.roll(ext, n - rho, axis=0)
            acc = acc + w_ref[j:j + 1, :] * rolled[rho][q * SUBLANES:q * SUBLANES + tm]
        return acc

    def conf_branch():
        acc = dwconv(extended(u0p_ref, u0_ref, u0n_ref), cw_ref, CONF_K,
                     jnp.zeros((tm, CONF_W), F32) + cb_ref[...])
        mu = jnp.mean(acc, axis=-1, keepdims=True)
        xc = acc - mu
        var = jnp.mean(xc * xc, axis=-1, keepdims=True)
        yn = xc * lax.rsqrt(var + EPS) * lg_ref[...] + lb_ref[...]
        return (yn * _sigmoid(yn)).astype(BF16)

    def sconv_branch():
        conv = dwconv(extended(cxp_ref, cx_ref, cxn_ref), scw_ref, SC_K,
                      jnp.zeros((tm, SC_W), F32))
        return (pb_ref[0].astype(F32) * conv).astype(BF16)

    ys = {0: lambda: ya_ref[0], 3: lambda: yf_ref[0].astype(BF16), 2: sconv_branch, 1: conf_branch}
    wbs = (wba_ref, wbc_ref, wbs_ref, wbf_ref)
    order = (0, 3, 2, 1)
    chunks = range(0, D_MODEL, MERGE_CHUNK)
    items = [(idx, n0) for idx in order for n0 in chunks]
    y_val = {}

    def issue(item):
        idx, n0 = item
        if idx not in y_val:
            y_val[idx] = ys[idx]()
        c0 = MAIN_W + idx * D_MODEL + n0
        return (_dot(hb, wg_ref[:, c0:c0 + MERGE_CHUNK]),
                _dot(y_val[idx], wbs[idx][:, n0:n0 + MERGE_CHUNK]))

    m = {}
    nxt = issue(items[0])
    y_val[2] = ys[2]()
    y_val[1] = ys[1]()
    for k, (idx, n0) in enumerate(items):
        pre, br = nxt
        if k + 1 < len(items):
            nxt = issue(items[k + 1])
        term = _sigmoid(pre) * br
        m[n0] = term if idx == order[0] else m[n0] + term
    mb = jnp.concatenate([m[n0].astype(BF16) for n0 in chunks], axis=1)
    o_ref[0] = x + _mod_row(gt_ref, mod_row) * _dot(mb, wo_ref[...])


def _merge_call(x, mods, g, ya, u0, cx, pb, yf, weights, l, *, tm, mod_row=None):
    b, s, d = x.shape
    nt = s // tm
    hpt = tm // HALO
    nh = s // HALO

    def tile(width):
        return pl.BlockSpec((1, tm, width), lambda bi, i: (bi, i, 0))

    def halo(width, off):
        if off < 0:
            return pl.BlockSpec((1, HALO, width),
                                lambda bi, i: (bi, jnp.maximum(i * hpt - 1, 0), 0))
        return pl.BlockSpec((1, HALO, width),
                            lambda bi, i: (bi, jnp.minimum((i + 1) * hpt, nh - 1), 0))

    in_specs = [tile(d), _mod_spec(mods, l, 0), _mod_spec(mods, l, 1), _mod_spec(mods, l, 2),
                _layer_spec(g, l), tile(Q_W),
                halo(CONF_W, -1), tile(CONF_W), halo(CONF_W, 1),
                halo(SC_W, -1), tile(SC_W), halo(SC_W, 1), tile(SC_W), tile(FN_W)]
    in_specs += [_layer_spec(w, l) for w in weights]
    return pl.pallas_call(
        functools.partial(_merge_kernel, tm=tm, nt=nt, mod_row=mod_row),
        grid=(b, nt),
        in_specs=in_specs,
        out_specs=tile(d),
        out_shape=jax.ShapeDtypeStruct((b, s, d), F32),
        compiler_params=_cparams("parallel", "parallel"),
        name="merge",
    )(x, mods, mods, mods, g, ya, u0, u0, u0, cx, cx, cx, pb, yf, *weights)


def _ffn_kernel(*refs, tm, nt, final, mod_row):
    if final:
        (xp_ref, x_ref, xn_ref, sh_ref, sc_ref, gt_ref, g_ref, wu_ref, wd_ref, dw_ref, db_ref,
         gf_ref, o_ref, gact_ref) = refs
    else:
        (xp_ref, x_ref, xn_ref, sh_ref, sc_ref, gt_ref, g_ref, wu_ref, wd_ref, dw_ref, db_ref,
         o_ref, gact_ref) = refs
    i = pl.program_id(1)
    x = x_ref[0]
    ne = tm + 2 * SUBLANES
    xe = jnp.concatenate([xp_ref[0], x, xn_ref[0]], axis=0)
    he = _norm_mod(xe, g_ref[...], _mod_row(sh_ref, mod_row), _mod_row(sc_ref, mod_row))
    rid = lax.broadcasted_iota(jnp.int32, (ne, 1), 0)
    lo = jnp.where(i > 0, 0, SUBLANES)
    hi = jnp.where(i < nt - 1, ne, tm + SUBLANES)
    he = jnp.where(jnp.logical_and(rid >= lo, rid < hi), he, 0.0).astype(BF16)

    mid = slice(SUBLANES, SUBLANES + tm)
    def up(c):
        return (_dot(he, wu_ref[:, c:c + FFN_CHUNK]),
                _dot(he, wu_ref[:, D_FF + c:D_FF + c + FFN_CHUNK])[mid])

    nxt = up(0)
    for c in range(0, D_FF, FFN_CHUNK):
        a, bb = nxt
        if c + FFN_CHUNK < D_FF:
            nxt = up(c + FFN_CHUNK)
        conv = (db_ref[:, c:c + FFN_CHUNK]
                + dw_ref[0:1, c:c + FFN_CHUNK] * pltpu.roll(a, 1, axis=0)[mid]
                + dw_ref[1:2, c:c + FFN_CHUNK] * a[mid]
                + dw_ref[2:3, c:c + FFN_CHUNK] * pltpu.roll(a, ne - 1, axis=0)[mid])
        gact_ref[:, c:c + FFN_CHUNK] = (conv * _sigmoid(conv) * bb).astype(BF16)
    out = x + _mod_row(gt_ref, mod_row) * _dot(gact_ref[...], wd_ref[...])
    if final:
        ms = jnp.mean(out * out, axis=-1, keepdims=True)
        out = out * lax.rsqrt(ms + EPS) * gf_ref[...]
    o_ref[0] = out


def _ffn_call(x, mods, g, wu, wd, dw, db, g_final, l, *, tm, mod_row=None):
    b, s, d = x.shape
    nt = s // tm
    hpt = tm // SUBLANES
    nh = s // SUBLANES
    final = g_final is not None
    in_specs = [
        pl.BlockSpec((1, SUBLANES, d), lambda bi, i: (bi, jnp.maximum(i * hpt - 1, 0), 0)),
        pl.BlockSpec((1, tm, d), lambda bi, i: (bi, i, 0)),
        pl.BlockSpec((1, SUBLANES, d), lambda bi, i: (bi, jnp.minimum((i + 1) * hpt, nh - 1), 0)),
        _mod_spec(mods, l, 3), _mod_spec(mods, l, 4), _mod_spec(mods, l, 5), _layer_spec(g, l),
        _layer_spec(wu, l), _layer_spec(wd, l), _layer_spec(dw, l), _layer_spec(db, l),
    ]
    args = [x, x, x, mods, mods, mods, g, wu, wd, dw, db]
    if final:
        in_specs.append(_const_spec((1, d)))
        args.append(g_final)
    return pl.pallas_call(
        functools.partial(_ffn_kernel, tm=tm, nt=nt, final=final, mod_row=mod_row),
        grid=(b, nt),
        in_specs=in_specs,
        out_specs=pl.BlockSpec((1, tm, d), lambda bi, i: (bi, i, 0)),
        out_shape=jax.ShapeDtypeStruct((b, s, d), F32),
        scratch_shapes=[pltpu.VMEM((tm, D_FF), BF16)],
        compiler_params=_cparams("parallel", "parallel"),
        name="ffn_final" if final else "ffn",
    )(*args)


def _rope_tables(s):
    n = np.arange(s)
    pos = np.stack([n // GRID_W, n % GRID_W], axis=0).astype(np.float64)
    inv = ROPE_BASE ** (-np.arange(ROPE_FREQS, dtype=np.float64) / ROPE_FREQS)
    lane = np.arange(LANES)
    axis = (lane % HEAD_DIM) // (2 * ROPE_FREQS)
    ang = pos[axis].T * inv[lane % ROPE_FREQS][None, :]
    first = (lane % (2 * ROPE_FREQS)) < ROPE_FREQS
    sin = np.sin(ang)
    return (np.cos(ang).astype(np.float32),
            np.where(first[None, :], -sin, 0.0).astype(np.float32),
            np.where(first[None, :], 0.0, sin).astype(np.float32))


def _pad_rows(w, rows):
    return jnp.pad(w, ((0, 0), (0, rows - w.shape[1]), (0, 0)))


def _vec(v):
    return v[:, None, :]


def kernel(x, c, ctx, c_ctx, w_ada, b_ada, g_mix, w_in, attn_sink, conf_dw_w, conf_dw_b, conf_ln_g,
           conf_ln_b, sc_w, w_br_attn, w_br_conf, w_br_sconv, w_br_fourier, w_out, g_ffn, w_up,
           ffn_dw_w, ffn_dw_b, w_down, g_final):
    b, s, d = x.shape
    lc = ctx.shape[1]
    tm = 512

    cvec = jnp.concatenate([c, c_ctx[None, :], jnp.zeros((SUBLANES - b - 1, d), F32)], axis=0)
    mods = _ada_call(cvec, w_ada, b_ada)
    ctx_row = b

    rope_tabs = tuple(jnp.asarray(t) for t in _rope_tables(s))
    m1, m2, twc, tws = _dft_consts(s)
    ch = jnp.asarray(_chan_consts()).astype(BF16)
    fft_consts = (jnp.asarray(m1).astype(BF16), jnp.asarray(m2).astype(BF16),
                  jnp.asarray(twc), jnp.asarray(tws), ch)
    m_ctx = jnp.asarray(_small_dft_const(lc)).astype(BF16)

    w_in_b = w_in.astype(BF16)
    g1, g2 = _vec(g_mix), _vec(g_ffn)
    sink_b = jnp.broadcast_to(attn_sink[:, :, None], attn_sink.shape + (LANES,))
    merge_w = (w_in_b, w_br_attn.astype(BF16), w_br_conf.astype(BF16), w_br_sconv.astype(BF16),
               w_br_fourier.astype(BF16), w_out.astype(BF16),
               _pad_rows(conf_dw_w, 32), _vec(conf_dw_b), _vec(conf_ln_g), _vec(conf_ln_b),
               _pad_rows(sc_w, SUBLANES))
    wu, wd = w_up.astype(BF16), w_down.astype(BF16)
    dw, db = _pad_rows(ffn_dw_w, SUBLANES), _vec(ffn_dw_b)

    for l in range(DEPTH):
        last = l == DEPTH - 1
        q, k, v, u0, pb, cx, fn = _proj_call(x, mods, g1, w_in_b, rope_tabs, l, tm=tm)
        if last:
            kc, vc = _proj_call(ctx, mods, g1, w_in_b, None, l, tm=lc, mod_row=ctx_row, kv_only=True)
        else:
            qc, kc, vc, u0c, pbc, cxc, fnc = _proj_call(ctx, mods, g1, w_in_b, None, l, tm=lc,
                                                        mod_row=ctx_row)

        ya = _attn_call(q, k, v, kc, vc, sink_b, l)
        yf = _fft_call(fn, fft_consts)
        x = _merge_call(x, mods, g1, ya, u0, cx, pb, yf, merge_w, l, tm=tm)
        if not last:
            yac = _attn_ctx_call(qc, kc, vc, sink_b, l)
            yfc = _fft_small_call(fnc, m_ctx, ch)
            ctx = _merge_call(ctx, mods, g1, yac, u0c, cxc, pbc, yfc, merge_w, l, tm=lc,
                              mod_row=ctx_row)

        x = _ffn_call(x, mods, g2, wu, wd, dw, db, g_final[None, :] if last else None, l, tm=tm)
        if not last:
            ctx = _ffn_call(ctx, mods, g2, wu, wd, dw, db, None, l, tm=lc, mod_row=ctx_row)
    return x
```

```python
import functools

import numpy as np
import jax
import jax.numpy as jnp
from jax import lax
from jax.experimental import pallas as pl
from jax.experimental.pallas import tpu as pltpu

F32 = jnp.float32
BF16 = jnp.bfloat16

D_MODEL = 1024
DEPTH = 4
GRID_W = 64
N_HEADS = 8
N_KV_HEADS = 2
HEAD_DIM = 64
WINDOW = 128
BLOCK = 128
ROPE_BASE = 10000.0
ROPE_FREQS = HEAD_DIM // 4
CONF_W = 256
CONF_K = 31
SC_W = 256
SC_K = 3
FN_W = 256
FN_GROUP_W = 64
N_BRANCH = 4
Q_W = N_HEADS * HEAD_DIM
KV_W = N_KV_HEADS * HEAD_DIM
MAIN_W = Q_W + 2 * KV_W + 2 * CONF_W + 3 * SC_W + FN_W
D_FF = 2816
EPS = 1e-6
NEG_INF = -1e30
LOG2E = 1.4426950408889634

LANES = 128
SUBLANES = 8
BF16_ROWS = 16
VMEM_LIMIT = 56 * 1024 * 1024
FFN_CHUNK = 256
MERGE_CHUNK = 256
ATTN_QB = 4
TILE_ROWS = 512
WIDE_TILE_ROWS = 1024


def _cparams(*sem):
    return pltpu.CompilerParams(dimension_semantics=sem, vmem_limit_bytes=VMEM_LIMIT)


def _const_spec(shape):
    nd = len(shape)
    return pl.BlockSpec(shape, lambda *_: (0,) * nd, pipeline_mode=pl.Buffered(1))


def _sigmoid(x):
    return 1.0 / (1.0 + jnp.exp(-x))


def _norm_mod(x, g, shift, scale):
    ms = jnp.mean(x * x, axis=-1, keepdims=True)
    y = x * lax.rsqrt(ms + EPS) * g
    return y * (1.0 + scale) + shift


def _dot(a, b):
    return jnp.dot(a, b, preferred_element_type=F32)


def _ada_kernel(c_ref, w_ref, b_ref, o_ref):
    c = c_ref[...]
    a = (c * _sigmoid(c)).astype(BF16)
    o_ref[...] = _dot(a, w_ref[...].astype(BF16)) + b_ref[...]


def _ada_call(cvec, w_ada, b_ada):
    depth, d, n = w_ada.shape
    nk = n // d
    return pl.pallas_call(
        _ada_kernel,
        grid=(depth, nk),
        in_specs=[
            pl.BlockSpec((SUBLANES, d), lambda l, k: (0, 0)),
            pl.BlockSpec((None, d, d), lambda l, k: (l, 0, k)),
            pl.BlockSpec((None, None, 1, d), lambda l, k: (l, k, 0, 0)),
        ],
        out_specs=pl.BlockSpec((None, None, SUBLANES, d), lambda l, k: (l, k, 0, 0)),
        out_shape=jax.ShapeDtypeStruct((depth, nk, SUBLANES, d), F32),
        compiler_params=_cparams("parallel", "parallel"),
        name="adaln",
    )(cvec, w_ada, b_ada.reshape(depth, nk, 1, d))


def _layer_spec(arr, l):
    return pl.BlockSpec((None,) + arr.shape[1:], lambda *_: (l, 0, 0), pipeline_mode=pl.Buffered(1))


def _mod_spec(mods, l, k):
    return pl.BlockSpec((None, None) + mods.shape[2:], lambda *_: (l, k, 0, 0))


def _mod_row(ref, row):
    r = pl.program_id(0) if row is None else row
    return ref[pl.ds(r, 1), :]


def _dup_halves(t):
    r = pltpu.roll(t, 64, axis=1)
    lo = lax.broadcasted_iota(jnp.int32, t.shape, 1) < 64
    return jnp.concatenate([jnp.where(lo, t, r), jnp.where(lo, r, t)], axis=1)


def _rope(t, cos, sin_a, sin_b):
    out = []
    for j in range(t.shape[1] // LANES):
        blk = t[:, j * LANES:(j + 1) * LANES]
        out.append(blk * cos + pltpu.roll(blk, LANES - 16, axis=1) * sin_a
                   + pltpu.roll(blk, 16, axis=1) * sin_b)
    return jnp.concatenate(out, axis=1) if len(out) > 1 else out[0]


def _proj_kernel(*refs, rope, kv_only, mod_row):
    if rope:
        x_ref, sh_ref, sc_ref, g_ref, w_ref, cos_ref, sa_ref, sb_ref = refs[:8]
        outs = refs[8:]
    else:
        x_ref, sh_ref, sc_ref, g_ref, w_ref = refs[:5]
        outs = refs[5:]
    hb = _norm_mod(x_ref[0], g_ref[...], _mod_row(sh_ref, mod_row),
                   _mod_row(sc_ref, mod_row)).astype(BF16)

    if kv_only:
        k_ref, v_ref = outs
        kv = _dot(hb, w_ref[:, Q_W:Q_W + 2 * KV_W])
        k_ref[0] = _dup_halves(kv[:, :KV_W]).astype(BF16)
        v_ref[0] = _dup_halves(kv[:, KV_W:]).astype(BF16)
        return

    q_ref, k_ref, v_ref, u0_ref, pb_ref, cx_ref, fn_ref, hb_ref = outs
    hb_ref[0] = hb
    qkv = _dot(hb, w_ref[:, 0:Q_W + 2 * KV_W])
    q = qkv[:, :Q_W]
    k = qkv[:, Q_W:Q_W + KV_W]
    v = qkv[:, Q_W + KV_W:]
    if rope:
        cos, sa, sb = cos_ref[...], sa_ref[...], sb_ref[...]
        q = _rope(q, cos, sa, sb)
        k = _rope(k, cos, sa, sb)
    q_ref[0] = (q * (HEAD_DIM ** -0.5 * LOG2E)).astype(BF16)
    k_ref[0] = _dup_halves(k).astype(BF16)
    v_ref[0] = _dup_halves(v).astype(BF16)

    c0 = Q_W + 2 * KV_W
    conf = _dot(hb, w_ref[:, c0:c0 + 2 * CONF_W])
    u0_ref[0] = (conf[:, :CONF_W] * _sigmoid(conf[:, CONF_W:])).astype(BF16)
    c0 += 2 * CONF_W
    pb_ref[0] = _dot(hb, w_ref[:, c0:c0 + SC_W]).astype(BF16)
    c0 += SC_W
    pcx = _dot(hb, w_ref[:, c0:c0 + 2 * SC_W])
    cx_ref[0] = (pcx[:, :SC_W] * pcx[:, SC_W:]).astype(BF16)
    c0 += 2 * SC_W
    fn_ref[0] = _dot(hb, w_ref[:, c0:c0 + FN_W])


def _proj_call(x, mods, g, w_in, rope_tabs, l, *, tm, mod_row=None, kv_only=False):
    b, s, d = x.shape
    nt = s // tm
    rope = rope_tabs is not None
    in_specs = [
        pl.BlockSpec((1, tm, d), lambda bi, i: (bi, i, 0)),
        _mod_spec(mods, l, 0), _mod_spec(mods, l, 1),
        _layer_spec(g, l),
        _layer_spec(w_in, l),
    ]
    args = [x, mods, mods, g, w_in]
    if rope:
        in_specs += [pl.BlockSpec((tm, LANES), lambda bi, i: (i, 0))] * 3
        args += list(rope_tabs)

    def out(width, dtype):
        return (pl.BlockSpec((1, tm, width), lambda bi, i: (bi, i, 0)),
                jax.ShapeDtypeStruct((b, s, width), dtype))

    if kv_only:
        outs = [out(2 * KV_W, BF16), out(2 * KV_W, BF16)]
    else:
        outs = [out(Q_W, BF16), out(2 * KV_W, BF16), out(2 * KV_W, BF16), out(CONF_W, BF16),
                out(SC_W, BF16), out(SC_W, BF16), out(FN_W, F32), out(d, BF16)]
    return pl.pallas_call(
        functools.partial(_proj_kernel, rope=rope, kv_only=kv_only, mod_row=mod_row),
        grid=(b, nt),
        in_specs=in_specs,
        out_specs=[o[0] for o in outs],
        out_shape=[o[1] for o in outs],
        compiler_params=_cparams("parallel", "parallel"),
        name="proj_kv" if kv_only else ("proj_rope" if rope else "proj"),
    )(*args)


def _attend(q, kd, vd, mask, sink_ref):
    scores = [_scores(q, kd, hk) for hk in range(N_KV_HEADS)]
    blocks = []
    for hk in range(N_KV_HEADS):
        blocks += _softmax_pv(scores[hk], mask, vd, hk, sink_ref)
    return jnp.concatenate(blocks, axis=1)


def _scores(q, kd, hk):
    rows = q.shape[0]
    lo = lax.broadcasted_iota(jnp.int32, (rows, LANES), 1) < 64
    qf = q.astype(F32)
    qs = []
    for p in range(N_HEADS // N_KV_HEADS // 2):
        blk = qf[:, (2 * hk + p) * LANES:(2 * hk + p + 1) * LANES]
        qs.append(jnp.where(lo, blk, 0.0))
        qs.append(jnp.where(lo, 0.0, blk))
    qst = jnp.concatenate(qs, axis=0).astype(BF16)
    kh = kd[:, hk * LANES:(hk + 1) * LANES]
    return lax.dot_general(qst, kh, (((1,), (1,)), ((), ())), preferred_element_type=F32)


def _softmax_pv(s, mask, vd, hk, sink_ref):
    group = N_HEADS // N_KV_HEADS
    rows = s.shape[0] // group
    lo = lax.broadcasted_iota(jnp.int32, (rows, LANES), 1) < 64
    sk = LOG2E * jnp.concatenate(
        [jnp.broadcast_to(sink_ref[group * hk + g:group * hk + g + 1, 0:1], (rows, 1))
         for g in range(group)], axis=0)
    if mask is not None:
        ok_prev, ok_next = mask
        s = jnp.concatenate([jnp.where(ok_prev, s[:, :BLOCK], NEG_INF), s[:, BLOCK:2 * BLOCK],
                             jnp.where(ok_next, s[:, 2 * BLOCK:3 * BLOCK], NEG_INF),
                             s[:, 3 * BLOCK:]], axis=1)
    m = jnp.maximum(jnp.max(s, axis=-1, keepdims=True), sk)
    p_ = jnp.exp2(s - m)
    denom = jnp.sum(p_, axis=-1, keepdims=True) + jnp.exp2(sk - m)
    o = _dot(p_.astype(BF16), vd[:, hk * LANES:(hk + 1) * LANES]) / denom
    return [jnp.where(lo, o[0:rows], o[rows:2 * rows]),
            jnp.where(lo, o[2 * rows:3 * rows], o[3 * rows:4 * rows])]


def _attn_kernel(q_ref, kp_ref, kc_ref, kn_ref, vp_ref, vc_ref, vn_ref, kx_ref, vx_ref, sink_ref,
                 o_ref, *, nb):
    i = pl.program_id(1)
    group = N_HEADS // N_KV_HEADS
    shape = (group * BLOCK, BLOCK)
    row = lax.broadcasted_iota(jnp.int32, shape, 0) & (BLOCK - 1)
    col = lax.broadcasted_iota(jnp.int32, shape, 1)
    kall = jnp.concatenate([kp_ref[0], kc_ref[0], kn_ref[0]], axis=0)
    vall = jnp.concatenate([vp_ref[0], vc_ref[0], vn_ref[0]], axis=0)
    scores = {}
    for sb in range(ATTN_QB):
        kd = jnp.concatenate([kall[sb * BLOCK:(sb + 3) * BLOCK], kx_ref[0]], axis=0)
        for hk in range(N_KV_HEADS):
            scores[sb, hk] = _scores(q_ref[0, sb * BLOCK:(sb + 1) * BLOCK, :], kd, hk)
    for sb in range(ATTN_QB):
        blk = i * ATTN_QB + sb
        off_p = jnp.where(blk > 0, 0, BLOCK)
        off_n = jnp.where(blk < nb - 1, 0, BLOCK)
        mask = (col >= row + off_p, col <= row - off_n)
        vd = jnp.concatenate([vall[sb * BLOCK:(sb + 3) * BLOCK], vx_ref[0]], axis=0)
        blocks = []
        for hk in range(N_KV_HEADS):
            blocks += _softmax_pv(scores[sb, hk], mask, vd, hk, sink_ref)
        o_ref[0, sb * BLOCK:(sb + 1) * BLOCK, :] = jnp.concatenate(blocks, axis=1).astype(BF16)


def _attn_call(q, k, v, kx, vx, sink_b, l):
    b, s, _ = q.shape
    nb = s // BLOCK
    lc = kx.shape[1]
    kvw = 2 * KV_W
    qb = ATTN_QB
    assert nb % qb == 0

    def edge_spec(off):
        def idx(bi, i):
            return (bi, jnp.clip(i * qb + off, 0, nb - 1), 0)
        return pl.BlockSpec((1, BLOCK, kvw), idx)

    main_spec = pl.BlockSpec((1, qb * BLOCK, kvw), lambda bi, i: (bi, i, 0))
    ctx_spec = pl.BlockSpec((1, lc, kvw), lambda bi, i: (bi, 0, 0))
    return pl.pallas_call(
        functools.partial(_attn_kernel, nb=nb),
        grid=(b, nb // qb),
        in_specs=[pl.BlockSpec((1, qb * BLOCK, Q_W), lambda bi, i: (bi, i, 0)),
                  edge_spec(-1), main_spec, edge_spec(qb), edge_spec(-1), main_spec, edge_spec(qb),
                  ctx_spec, ctx_spec, _layer_spec(sink_b, l)],
        out_specs=pl.BlockSpec((1, qb * BLOCK, Q_W), lambda bi, i: (bi, i, 0)),
        out_shape=jax.ShapeDtypeStruct((b, s, Q_W), BF16),
        compiler_params=_cparams("parallel", "parallel"),
        name="attn_window",
    )(q, k, k, k, v, v, v, kx, vx, sink_b)


def _attn_ctx_kernel(q_ref, k_ref, v_ref, sink_ref, o_ref):
    o_ref[0] = _attend(q_ref[0], k_ref[0], v_ref[0], None, sink_ref).astype(BF16)


def _attn_ctx_call(q, k, v, sink_b, l):
    b, lc, _ = q.shape
    kvw = 2 * KV_W
    return pl.pallas_call(
        _attn_ctx_kernel,
        grid=(b,),
        in_specs=[pl.BlockSpec((1, lc, Q_W), lambda bi: (bi, 0, 0)),
                  pl.BlockSpec((1, lc, kvw), lambda bi: (bi, 0, 0)),
                  pl.BlockSpec((1, lc, kvw), lambda bi: (bi, 0, 0)),
                  _layer_spec(sink_b, l)],
        out_specs=pl.BlockSpec((1, lc, Q_W), lambda bi: (bi, 0, 0)),
        out_shape=jax.ShapeDtypeStruct((b, lc, Q_W), BF16),
        compiler_params=_cparams("parallel"),
        name="attn_ctx",
    )(q, k, v, sink_b)


def _dft_consts(n_pos):
    r = int(round(np.sqrt(n_pos)))
    assert n_pos == r * r and r % SUBLANES == 0
    idx = np.arange(r)
    ang = 2.0 * np.pi * np.outer(idx, idx) / r
    c, s = np.cos(ang), np.sin(ang)
    m1 = np.concatenate([c, -s], axis=0)
    m2 = np.block([[c, s], [-s, c]])
    tw = 2.0 * np.pi * np.outer(idx, idx) / n_pos
    twc = np.broadcast_to(np.cos(tw)[:, :, None], (r, r, LANES))
    tws = np.broadcast_to(np.sin(tw)[:, :, None], (r, r, LANES))
    return (m1.astype(np.float32), m2.astype(np.float32),
            np.ascontiguousarray(twc, np.float32), np.ascontiguousarray(tws, np.float32))


def _chan_consts():
    idx = np.arange(FN_GROUP_W)
    ang = 2.0 * np.pi * np.outer(idx, idx) / FN_GROUP_W
    groups = FN_W // FN_GROUP_W
    cc = np.kron(np.eye(groups), np.cos(ang))
    sc = np.kron(np.eye(groups), np.sin(ang))
    return np.concatenate([cc, sc], axis=0).astype(np.float32)


def _fft1_kernel(x_ref, m1_ref, twc_ref, tws_ref, o_ref):
    r = x_ref.shape[1]
    ns = x_ref.shape[2]
    xt = pltpu.einshape("asc->sac", x_ref[0])
    trs, tis = [], []
    for sl in range(ns):
        t = _dot(m1_ref[...], xt[sl].astype(BF16))
        tr, ti = t[:r], t[r:]
        cw = jnp.concatenate([twc_ref[sl]] * (FN_W // LANES), axis=1)
        sw = jnp.concatenate([tws_ref[sl]] * (FN_W // LANES), axis=1)
        trs.append(tr * cw + ti * sw)
        tis.append(ti * cw - tr * sw)
    o_ref[0, 0] = pltpu.einshape("skc->ksc", jnp.stack(trs, axis=0)).astype(BF16)
    o_ref[0, 1] = pltpu.einshape("skc->ksc", jnp.stack(tis, axis=0)).astype(BF16)


def _fft2_kernel(t_ref, m2_ref, ch_ref, o_ref, *, scale):
    r = t_ref.shape[3]
    ys = []
    for kl in range(SUBLANES):
        t = jnp.concatenate([t_ref[0, 0, kl], t_ref[0, 1, kl]], axis=0)
        z = _dot(m2_ref[...], t)
        zc = jnp.concatenate([z[:r], z[r:]], axis=1).astype(BF16)
        ys.append(_dot(zc, ch_ref[...]) * scale)
    o_ref[0] = pltpu.einshape("kjc->jkc", jnp.stack(ys, axis=0))


def _fft_call(fn, consts):
    m1, m2, twc, tws, ch = consts
    b, n, w = fn.shape
    r = m1.shape[1]
    nblk = r // SUBLANES
    ns = BF16_ROWS
    x4 = fn.reshape(b, r, r, w)
    t = pl.pallas_call(
        _fft1_kernel,
        grid=(b, r // ns),
        in_specs=[pl.BlockSpec((1, r, ns, w), lambda bi, j: (bi, 0, j, 0)),
                  _const_spec(m1.shape),
                  pl.BlockSpec((ns, r, LANES), lambda bi, j: (j, 0, 0)),
                  pl.BlockSpec((ns, r, LANES), lambda bi, j: (j, 0, 0))],
        out_specs=pl.BlockSpec((1, 2, r, ns, w), lambda bi, j: (bi, 0, 0, j, 0)),
        out_shape=jax.ShapeDtypeStruct((b, 2, r, r, w), BF16),
        compiler_params=_cparams("parallel", "parallel"),
        name="fft_stage1",
    )(x4, m1, twc, tws)
    y = pl.pallas_call(
        functools.partial(_fft2_kernel, scale=float(1.0 / np.sqrt(n * FN_GROUP_W))),
        grid=(b, nblk),
        in_specs=[pl.BlockSpec((1, 2, SUBLANES, r, w), lambda bi, j: (bi, 0, j, 0, 0)),
                  _const_spec(m2.shape),
                  _const_spec(ch.shape)],
        out_specs=pl.BlockSpec((1, r, SUBLANES, w), lambda bi, j: (bi, 0, j, 0)),
        out_shape=jax.ShapeDtypeStruct((b, r, r, w), F32),
        compiler_params=_cparams("parallel", "parallel"),
        name="fft_stage2",
    )(t, m2, ch)
    return y.reshape(b, n, w)


def _fft_small_kernel(x_ref, m_ref, ch_ref, o_ref, *, scale):
    n = x_ref.shape[1]
    z = _dot(m_ref[...], x_ref[0].astype(BF16))
    zc = jnp.concatenate([z[:n], z[n:]], axis=1).astype(BF16)
    o_ref[0] = _dot(zc, ch_ref[...]) * scale


def _fft_small_call(fn, m, ch):
    b, n, w = fn.shape
    return pl.pallas_call(
        functools.partial(_fft_small_kernel, scale=float(1.0 / np.sqrt(n * FN_GROUP_W))),
        grid=(b,),
        in_specs=[pl.BlockSpec((1, n, w), lambda bi: (bi, 0, 0)),
                  _const_spec(m.shape), _const_spec(ch.shape)],
        out_specs=pl.BlockSpec((1, n, w), lambda bi: (bi, 0, 0)),
        out_shape=jax.ShapeDtypeStruct((b, n, w), F32),
        compiler_params=_cparams("parallel"),
        name="fft_ctx",
    )(fn, m, ch)


def _small_dft_const(n):
    idx = np.arange(n)
    ang = 2.0 * np.pi * np.outer(idx, idx) / n
    return np.concatenate([np.cos(ang), -np.sin(ang)], axis=0).astype(np.float32)


HALO = 16


def _merge_kernel(x_ref, hb_ref, gt_ref, ya_ref,
                  u0p_ref, u0_ref, u0n_ref, cxp_ref, cx_ref, cxn_ref, pb_ref, yf_ref,
                  wg_ref, wba_ref, wbc_ref, wbs_ref, wbf_ref, wo_ref,
                  cw_ref, cb_ref, lg_ref, lb_ref, scw_ref,
                  o_ref, *, tm, nt, mod_row):
    i = pl.program_id(1)
    first = i == 0
    last = i == nt - 1

    def extended(p_ref, c_ref, n_ref):
        return jnp.concatenate([jnp.where(first, 0.0, p_ref[0].astype(F32)),
                                c_ref[0].astype(F32),
                                jnp.where(last, 0.0, n_ref[0].astype(F32))], axis=0)

    def dwconv(ext, w_ref, taps, init):
        n = ext.shape[0]
        pad = (taps - 1) // 2
        rolled = {}
        acc = init
        for j in range(taps):
            q, rho = divmod(HALO - pad + j, SUBLANES)
            if rho not in rolled:
                rolled[rho] = ext if rho == 0 else pltpu.roll(ext, n - rho, axis=0)
            acc = acc + w_ref[j:j + 1, :] * rolled[rho][q * SUBLANES:q * SUBLANES + tm]
        return acc

    def conf_branch():
        acc = dwconv(extended(u0p_ref, u0_ref, u0n_ref), cw_ref, CONF_K,
                     jnp.zeros((tm, CONF_W), F32) + cb_ref[...])
        mu = jnp.mean(acc, axis=-1, keepdims=True)
        xc = acc - mu
        var = jnp.mean(xc * xc, axis=-1, keepdims=True)
        yn = xc * lax.rsqrt(var + EPS) * lg_ref[...] + lb_ref[...]
        return (yn * _sigmoid(yn)).astype(BF16)

    def sconv_branch():
        conv = dwconv(extended(cxp_ref, cx_ref, cxn_ref), scw_ref, SC_K,
                      jnp.zeros((tm, SC_W), F32))
        return (pb_ref[0].astype(F32) * conv).astype(BF16)

    hh = hb_ref[0] * jnp.asarray(0.5, BF16)
    wbs = (wba_ref, wbc_ref, wbs_ref, wbf_ref)
    chunks = range(0, D_MODEL, MERGE_CHUNK)
    ys = {0: lambda: ya_ref[0], 3: lambda: yf_ref[0].astype(BF16), 2: sconv_branch, 1: conf_branch}
    order = (0, 3, 2, 1)
    items = [(idx, n0) for idx in order for n0 in chunks]
    y_val = {}

    def issue(item):
        idx, n0 = item
        if idx not in y_val:
            y_val[idx] = ys[idx]()
        c0 = MAIN_W + idx * D_MODEL + n0
        return (_dot(hh, wg_ref[:, c0:c0 + MERGE_CHUNK]),
                _dot(y_val[idx], wbs[idx][:, n0:n0 + MERGE_CHUNK]))

    m = {}
    nxt = issue(items[0])
    for k, (idx, n0) in enumerate(items):
        pre_half, br = nxt
        if k + 1 < len(items):
            nxt = issue(items[k + 1])
        term = (1.0 + jnp.tanh(pre_half)) * br
        m[n0] = term if idx == order[0] else m[n0] + term
    mb = jnp.concatenate([(0.5 * m[n0]).astype(BF16) for n0 in chunks], axis=1)
    o_ref[0] = x_ref[0] + _mod_row(gt_ref, mod_row) * _dot(mb, wo_ref[...])


def _merge_call(x, hb, mods, ya, u0, cx, pb, yf, weights, l, *, tm, mod_row=None):
    b, s, d = x.shape
    nt = s // tm
    hpt = tm // HALO
    nh = s // HALO

    def tile(width):
        return pl.BlockSpec((1, tm, width), lambda bi, i: (bi, i, 0))

    def halo(width, off):
        if off < 0:
            return pl.BlockSpec((1, HALO, width),
                                lambda bi, i: (bi, jnp.maximum(i * hpt - 1, 0), 0))
        return pl.BlockSpec((1, HALO, width),
                            lambda bi, i: (bi, jnp.minimum((i + 1) * hpt, nh - 1), 0))

    in_specs = [tile(d), tile(d), _mod_spec(mods, l, 2), tile(Q_W),
                halo(CONF_W, -1), tile(CONF_W), halo(CONF_W, 1),
                halo(SC_W, -1), tile(SC_W), halo(SC_W, 1), tile(SC_W), tile(FN_W)]
    in_specs += [_layer_spec(w, l) for w in weights]
    return pl.pallas_call(
        functools.partial(_merge_kernel, tm=tm, nt=nt, mod_row=mod_row),
        grid=(b, nt),
        in_specs=in_specs,
        out_specs=tile(d),
        out_shape=jax.ShapeDtypeStruct((b, s, d), F32),
        compiler_params=_cparams("parallel", "parallel"),
        name="merge",
    )(x, hb, mods, ya, u0, u0, u0, cx, cx, cx, pb, yf, *weights)


def _ffn_kernel(*refs, tm, nt, final, mod_row):
    if final:
        (xp_ref, x_ref, xn_ref, sh_ref, sc_ref, gt_ref, g_ref, wu_ref, wd_ref, dw_ref, db_ref,
         gf_ref, o_ref, gact_ref) = refs
    else:
        (xp_ref, x_ref, xn_ref, sh_ref, sc_ref, gt_ref, g_ref, wu_ref, wd_ref, dw_ref, db_ref,
         o_ref, gact_ref) = refs
    i = pl.program_id(1)
    x = x_ref[0]
    ne = tm + 2 * SUBLANES
    xe = jnp.concatenate([xp_ref[0], x, xn_ref[0]], axis=0)
    he = _norm_mod(xe, g_ref[...], _mod_row(sh_ref, mod_row), _mod_row(sc_ref, mod_row))
    rid = lax.broadcasted_iota(jnp.int32, (ne, 1), 0)
    lo = jnp.where(i > 0, 0, SUBLANES)
    hi = jnp.where(i < nt - 1, ne, tm + SUBLANES)
    he = jnp.where(jnp.logical_and(rid >= lo, rid < hi), he, 0.0).astype(BF16)

    mid = slice(SUBLANES, SUBLANES + tm)
    def up(c):
        return (_dot(he, wu_ref[:, c:c + FFN_CHUNK]),
                _dot(he, wu_ref[:, D_FF + c:D_FF + c + FFN_CHUNK])[mid])

    nxt = up(0)
    for c in range(0, D_FF, FFN_CHUNK):
        a, bb = nxt
        if c + FFN_CHUNK < D_FF:
            nxt = up(c + FFN_CHUNK)
        conv = (db_ref[:, c:c + FFN_CHUNK]
                + dw_ref[0:1, c:c + FFN_CHUNK] * pltpu.roll(a, 1, axis=0)[mid]
                + dw_ref[1:2, c:c + FFN_CHUNK] * a[mid]
                + dw_ref[2:3, c:c + FFN_CHUNK] * pltpu.roll(a, ne - 1, axis=0)[mid])
        gact_ref[:, c:c + FFN_CHUNK] = (conv * _sigmoid(conv) * bb).astype(BF16)
    out = x + _mod_row(gt_ref, mod_row) * _dot(gact_ref[...], wd_ref[...])
    if final:
        ms = jnp.mean(out * out, axis=-1, keepdims=True)
        out = out * lax.rsqrt(ms + EPS) * gf_ref[...]
    o_ref[0] = out


def _ffn_call(x, mods, g, wu, wd, dw, db, g_final, l, *, tm, mod_row=None):
    b, s, d = x.shape
    nt = s // tm
    hpt = tm // SUBLANES
    nh = s // SUBLANES
    final = g_final is not None
    in_specs = [
        pl.BlockSpec((1, SUBLANES, d), lambda bi, i: (bi, jnp.maximum(i * hpt - 1, 0), 0)),
        pl.BlockSpec((1, tm, d), lambda bi, i: (bi, i, 0)),
        pl.BlockSpec((1, SUBLANES, d), lambda bi, i: (bi, jnp.minimum((i + 1) * hpt, nh - 1), 0)),
        _mod_spec(mods, l, 3), _mod_spec(mods, l, 4), _mod_spec(mods, l, 5), _layer_spec(g, l),
        _layer_spec(wu, l), _layer_spec(wd, l), _layer_spec(dw, l), _layer_spec(db, l),
    ]
    args = [x, x, x, mods, mods, mods, g, wu, wd, dw, db]
    if final:
        in_specs.append(_const_spec((1, d)))
        args.append(g_final)
    return pl.pallas_call(
        functools.partial(_ffn_kernel, tm=tm, nt=nt, final=final, mod_row=mod_row),
        grid=(b, nt),
        in_specs=in_specs,
        out_specs=pl.BlockSpec((1, tm, d), lambda bi, i: (bi, i, 0)),
        out_shape=jax.ShapeDtypeStruct((b, s, d), F32),
        scratch_shapes=[pltpu.VMEM((tm, D_FF), BF16)],
        compiler_params=_cparams("parallel", "parallel"),
        name="ffn_final" if final else "ffn",
    )(*args)


def _rope_tables(s):
    n = np.arange(s)
    pos = np.stack([n // GRID_W, n % GRID_W], axis=0).astype(np.float64)
    inv = ROPE_BASE ** (-np.arange(ROPE_FREQS, dtype=np.float64) / ROPE_FREQS)
    lane = np.arange(LANES)
    axis = (lane % HEAD_DIM) // (2 * ROPE_FREQS)
    ang = pos[axis].T * inv[lane % ROPE_FREQS][None, :]
    first = (lane % (2 * ROPE_FREQS)) < ROPE_FREQS
    sin = np.sin(ang)
    return (np.cos(ang).astype(np.float32),
            np.where(first[None, :], -sin, 0.0).astype(np.float32),
            np.where(first[None, :], 0.0, sin).astype(np.float32))


def _pad_rows(w, rows):
    return jnp.pad(w, ((0, 0), (0, rows - w.shape[1]), (0, 0)))


def _vec(v):
    return v[:, None, :]


def kernel(x, c, ctx, c_ctx, w_ada, b_ada, g_mix, w_in, attn_sink, conf_dw_w, conf_dw_b, conf_ln_g,
           conf_ln_b, sc_w, w_br_attn, w_br_conf, w_br_sconv, w_br_fourier, w_out, g_ffn, w_up,
           ffn_dw_w, ffn_dw_b, w_down, g_final):
    b, s, d = x.shape
    lc = ctx.shape[1]
    tm = min(TILE_ROWS, s)
    tm_wide = min(WIDE_TILE_ROWS, s)

    cvec = jnp.concatenate([c, c_ctx[None, :], jnp.zeros((SUBLANES - b - 1, d), F32)], axis=0)
    mods = _ada_call(cvec, w_ada, b_ada)
    ctx_row = b

    rope_tabs = tuple(jnp.asarray(t) for t in _rope_tables(s))
    m1, m2, twc, tws = _dft_consts(s)
    ch = jnp.asarray(_chan_consts()).astype(BF16)
    fft_consts = (jnp.asarray(m1).astype(BF16), jnp.asarray(m2).astype(BF16),
                  jnp.asarray(twc), jnp.asarray(tws), ch)
    m_ctx = jnp.asarray(_small_dft_const(lc)).astype(BF16)

    w_in_b = w_in.astype(BF16)
    g1, g2 = _vec(g_mix), _vec(g_ffn)
    sink_b = jnp.broadcast_to(attn_sink[:, :, None], attn_sink.shape + (LANES,))
    merge_w = (w_in_b, w_br_attn.astype(BF16), w_br_conf.astype(BF16), w_br_sconv.astype(BF16),
               w_br_fourier.astype(BF16), w_out.astype(BF16),
               _pad_rows(conf_dw_w, 32), _vec(conf_dw_b), _vec(conf_ln_g), _vec(conf_ln_b),
               _pad_rows(sc_w, SUBLANES))
    wu, wd = w_up.astype(BF16), w_down.astype(BF16)
    dw, db = _pad_rows(ffn_dw_w, SUBLANES), _vec(ffn_dw_b)

    for l in range(DEPTH):
        last = l == DEPTH - 1
        q, k, v, u0, pb, cx, fn, hb = _proj_call(x, mods, g1, w_in_b, rope_tabs, l, tm=tm_wide)
        if last:
            kc, vc = _proj_call(ctx, mods, g1, w_in_b, None, l, tm=lc, mod_row=ctx_row, kv_only=True)
        else:
            qc, kc, vc, u0c, pbc, cxc, fnc, hbc = _proj_call(ctx, mods, g1, w_in_b, None, l, tm=lc,
                                                             mod_row=ctx_row)

        ya = _attn_call(q, k, v, kc, vc, sink_b, l)
        yf = _fft_call(fn, fft_consts)
        x = _merge_call(x, hb, mods, ya, u0, cx, pb, yf, merge_w, l, tm=tm)
        if not last:
            yac = _attn_ctx_call(qc, kc, vc, sink_b, l)
            yfc = _fft_small_call(fnc, m_ctx, ch)
            ctx = _merge_call(ctx, hbc, mods, yac, u0c, cxc, pbc, yfc, merge_w, l, tm=lc,
                              mod_row=ctx_row)

        x = _ffn_call(x, mods, g2, wu, wd, dw, db, g_final[None, :] if last else None, l,
                      tm=tm_wide)
        if not last:
            ctx = _ffn_call(ctx, mods, g2, wu, wd, dw, db, None, l, tm=lc, mod_row=ctx_row)
    return x
```
